```python
import jax, jax.numpy as jnp
from jax import lax
import numpy as np

D_MODEL = 2048
BATCH = 2
SEQ = 4096
DEPTH = 1
DEC_BATCH = 32
DEC_SEQ = 4
PAST_LEN = 8192
PAGE_SIZE = 128

N_HEADS = 8
HEAD_DIM = 128
D_ATTN = N_HEADS * HEAD_DIM
D_CONV = D_MODEL - D_ATTN
CONV_WIDTH = 3
MOBA_BLOCK = 256
MOBA_TOPK = 3
Q_CHUNK = 64
NORM_EPS = 1e-6
IN_SPLITS = (D_ATTN, D_ATTN, D_ATTN, D_ATTN, D_CONV, D_CONV, D_CONV, D_CONV, D_MODEL, D_MODEL)
D_IN = sum(IN_SPLITS)

kernel_name = "moba_shortconv_gated_hybrid_step"


def alibi_slopes():
    h = jnp.arange(1, N_HEADS + 1, dtype=jnp.float32)
    return jnp.exp2(-8.0 * h / N_HEADS)


def rmsnorm(x, g):
    xf = x.astype(jnp.float32)
    y = xf * lax.rsqrt(jnp.mean(xf * xf, axis=-1, keepdims=True) + NORM_EPS)
    return y.astype(x.dtype) * g


def mixer_in(x, c, norm_g, w_ada, b_ada, w_in):
    mod = (c @ w_ada + b_ada)[:, None, :]
    shift, scale, gate = jnp.split(mod, 3, axis=-1)
    h = rmsnorm(x, norm_g) * (1 + scale) + shift
    parts = jnp.split(h @ w_in, np.cumsum(IN_SPLITS)[:-1].tolist(), axis=-1)
    return gate, parts


def mixer_out(x, gate, attn, za, conv, zb, ga, gb, w_pa, w_pb, w_o):
    ya = (attn * jax.nn.silu(za)) @ w_pa
    yb = (conv * jax.nn.silu(zb)) @ w_pb
    merged = jax.nn.sigmoid(ga) * ya + jax.nn.sigmoid(gb) * yb
    return x + gate * (merged @ w_o)


def heads(t):
    return t.reshape(t.shape[0], t.shape[1], N_HEADS, HEAD_DIM)


def to_blocks(k):
    B, L, H, Dh = k.shape
    nb = -(-L // MOBA_BLOCK)
    k = jnp.pad(k, ((0, 0), (0, nb * MOBA_BLOCK - L), (0, 0), (0, 0)))
    return k.reshape(B, nb, MOBA_BLOCK, H, Dh)


def moba_attend(q, kb, vb, kmean, q_pos, slopes):
    B, Q, H, Dh = q.shape
    NB = kb.shape[1]
    neg = jnp.finfo(jnp.float32).min
    cur = q_pos // MOBA_BLOCK
    gate = jnp.einsum('bqhd,bnhd->bhqn', q.astype(jnp.float32), kmean)
    fully_past = jnp.arange(NB)[None, :] < cur[:, None]
    gate = jnp.where(fully_past[None, None], gate, neg)
    _, idx = lax.top_k(gate, min(MOBA_TOPK, NB))
    valid = idx < cur[None, None, :, None]
    own = jnp.broadcast_to(cur[None, None, :, None], (B, H, Q, 1)).astype(idx.dtype)
    sel = jnp.concatenate([idx, own], axis=-1)
    sel_ok = jnp.concatenate([valid, jnp.ones(own.shape, bool)], axis=-1)
    bi = jnp.arange(B)[:, None, None, None]
    hi = jnp.arange(H)[None, :, None, None]
    kg = kb[bi, sel, :, hi, :]
    vg = vb[bi, sel, :, hi, :]
    k_pos = sel[..., None] * MOBA_BLOCK + jnp.arange(MOBA_BLOCK, dtype=sel.dtype)
    dist = q_pos[None, None, :, None, None] - k_pos
    mask = sel_ok[..., None] & (dist >= 0)
    s = jnp.einsum('bqhd,bhqsld->bhqsl', q, kg).astype(jnp.float32) * (HEAD_DIM ** -0.5)
    s = s - slopes[None, :, None, None, None] * dist.astype(jnp.float32)
    s = jnp.where(mask, s, neg)
    p = jax.nn.softmax(s.reshape(B, H, Q, -1), axis=-1).reshape(s.shape)
    return jnp.einsum('bhqsl,bhqsld->bqhd', p.astype(vg.dtype), vg)


def moba_prompt(q, kb, vb, kmean, slopes):
    B, T, H, Dh = q.shape
    n_chunks = T // Q_CHUNK
    qc = q.reshape(B, n_chunks, Q_CHUNK, H, Dh).swapaxes(0, 1)
    pos = jnp.arange(T, dtype=jnp.int32).reshape(n_chunks, Q_CHUNK)
    o = lax.map(lambda a: moba_attend(a[0], kb, vb, kmean, a[1], slopes), (qc, pos))
    return o.swapaxes(0, 1).reshape(B, T, H, Dh)


def causal_conv(upad, w):
    T = upad.shape[1] - (CONV_WIDTH - 1)
    out = w[0] * upad[:, 0:T]
    for j in range(1, CONV_WIDTH):
        out = out + w[j] * upad[:, j:j + T]
    return out


def setup_inputs(seed: int = 0) -> dict:
    key = jax.random.key(seed)
    ks = jax.random.split(key, 20)
    f32 = jnp.float32
    n_pages = PAST_LEN // PAGE_SIZE
    n_used = DEC_BATCH * n_pages
    n_pool = n_used + n_used // 4

    def nrm(k, shape, scale=1.0):
        return jax.random.normal(k, shape, f32) * scale

    page_table = jax.random.permutation(ks[5], n_pool)[:n_used].reshape(DEC_BATCH, n_pages).astype(jnp.int32)
    return {
        "x_prompt": nrm(ks[0], (BATCH, SEQ, D_MODEL)),
        "x_sample": nrm(ks[1], (DEC_BATCH, DEC_SEQ, D_MODEL)),
        "cache_k": nrm(ks[2], (DEPTH, n_pool, PAGE_SIZE, N_HEADS, HEAD_DIM)),
        "cache_v": nrm(ks[3], (DEPTH, n_pool, PAGE_SIZE, N_HEADS, HEAD_DIM)),
        "state_conv": nrm(ks[4], (DEPTH, DEC_BATCH, CONV_WIDTH - 1, D_CONV)),
        "page_table": page_table,
        "c_prompt": nrm(ks[6], (BATCH, D_MODEL)),
        "c_sample": nrm(ks[7], (DEC_BATCH, D_MODEL)),
        "norm_g": 1.0 + nrm(ks[8], (DEPTH, D_MODEL), 0.1),
        "w_ada": nrm(ks[9], (DEPTH, D_MODEL, 3 * D_MODEL), 0.5 * D_MODEL ** -0.5),
        "b_ada": nrm(ks[10], (DEPTH, 3 * D_MODEL), 0.02),
        "w_in": nrm(ks[11], (DEPTH, D_MODEL, D_IN), D_MODEL ** -0.5),
        "conv_w": nrm(ks[12], (DEPTH, CONV_WIDTH, D_CONV), CONV_WIDTH ** -0.5),
        "w_pa": nrm(ks[13], (DEPTH, D_ATTN, D_MODEL), D_ATTN ** -0.5),
        "w_pb": nrm(ks[14], (DEPTH, D_CONV, D_MODEL), D_CONV ** -0.5),
        "w_o": nrm(ks[15], (DEPTH, D_MODEL, D_MODEL), D_MODEL ** -0.5),
        "final_g": 1.0 + nrm(ks[16], (D_MODEL,), 0.1),
    }


def reference(x_prompt, x_sample, cache_k, cache_v, state_conv, page_table, c_prompt, c_sample,
              norm_g, w_ada, b_ada, w_in, conv_w, w_pa, w_pb, w_o, final_g):
    slopes = alibi_slopes()
    past_len = page_table.shape[1] * cache_k.shape[2]
    xp, xs = x_prompt, x_sample
    kp_l, vp_l, cp_l, ks_l, vs_l, cs_l = [], [], [], [], [], []
    for l in range(DEPTH):
        gate, (q, k, v, za, bg, cg, u, zb, ga, gb) = mixer_in(xp, c_prompt, norm_g[l], w_ada[l], b_ada[l], w_in[l])
        q, k, v = heads(q), heads(k), heads(v)
        kb, vb = to_blocks(k), to_blocks(v)
        kmean = jnp.mean(kb.astype(jnp.float32), axis=2)
        attn = moba_prompt(q, kb, vb, kmean, slopes).reshape(xp.shape[0], xp.shape[1], D_ATTN)
        upad = jnp.pad(cg * u, ((0, 0), (CONV_WIDTH - 1, 0), (0, 0)))
        conv = bg * causal_conv(upad, conv_w[l])
        xp_next = mixer_out(xp, gate, attn, za, conv, zb, ga, gb, w_pa[l], w_pb[l], w_o[l])
        kp_l.append(k)
        vp_l.append(v)
        cp_l.append(upad[:, -(CONV_WIDTH - 1):])

        gate, (q, k, v, za, bg, cg, u, zb, ga, gb) = mixer_in(xs, c_sample, norm_g[l], w_ada[l], b_ada[l], w_in[l])
        q, k, v = heads(q), heads(k), heads(v)
        db, nq = xs.shape[0], xs.shape[1]
        k_past = cache_k[l][page_table].reshape(db, past_len, N_HEADS, HEAD_DIM)
        v_past = cache_v[l][page_table].reshape(db, past_len, N_HEADS, HEAD_DIM)
        kb = to_blocks(jnp.concatenate([k_past, k], axis=1))
        vb = to_blocks(jnp.concatenate([v_past, v], axis=1))
        kmean = jnp.mean(kb.astype(jnp.float32), axis=2)
        q_pos = past_len + jnp.arange(nq, dtype=jnp.int32)
        attn = moba_attend(q, kb, vb, kmean, q_pos, slopes).reshape(db, nq, D_ATTN)
        upad = jnp.concatenate([state_conv[l].astype(u.dtype), cg * u], axis=1)
        conv = bg * causal_conv(upad, conv_w[l])
        xs_next = mixer_out(xs, gate, attn, za, conv, zb, ga, gb, w_pa[l], w_pb[l], w_o[l])
        ks_l.append(k)
        vs_l.append(v)
        cs_l.append(upad[:, -(CONV_WIDTH - 1):])

        xp, xs = xp_next, xs_next
    y_prompt = rmsnorm(xp, final_g)
    y_sample = rmsnorm(xs, final_g)
    return (y_prompt, y_sample, jnp.stack(kp_l), jnp.stack(vp_l), jnp.stack(cp_l),
            jnp.stack(ks_l), jnp.stack(vs_l), jnp.stack(cs_l))
```

```python
import functools

import jax
import jax.numpy as jnp
from jax import lax
from jax.experimental import pallas as pl
from jax.experimental.pallas import tpu as pltpu

F32 = jnp.float32
BF16 = jnp.bfloat16

N_HEADS = 8
HEAD_DIM = 128
MOBA_BLOCK = 256
MOBA_TOPK = 3
CONV_WIDTH = 3
NORM_EPS = 1e-6
NEG = float(jnp.finfo(jnp.float32).min)
ATTN_SCALE = HEAD_DIM ** -0.5

VMEM_LIMIT_BYTES = 56 * 1024 * 1024

_NT = (((1,), (1,)), ((), ()))


def _params(*semantics):
    return pltpu.CompilerParams(dimension_semantics=semantics, vmem_limit_bytes=VMEM_LIMIT_BYTES)


def _ada_kernel(c_ref, w_ref, b_ref, o_ref):
    o_ref[...] = jnp.dot(c_ref[...].astype(BF16), w_ref[...].astype(BF16),
                         preferred_element_type=F32) + b_ref[...]


def _ada(c_all, w_ada, b_ada):
    rows, d = c_all.shape
    n = w_ada.shape[1]
    tn = 1024
    return pl.pallas_call(
        _ada_kernel,
        grid=(n // tn,),
        in_specs=[pl.BlockSpec((rows, d), lambda j: (0, 0)),
                  pl.BlockSpec((d, tn), lambda j: (0, j)),
                  pl.BlockSpec((1, tn), lambda j: (0, j))],
        out_specs=pl.BlockSpec((rows, tn), lambda j: (0, j)),
        out_shape=jax.ShapeDtypeStruct((rows, n), F32),
        compiler_params=_params("arbitrary"),
        name="ada",
    )(c_all, w_ada, b_ada)


def _prep_kernel(x_ref, g_ref, scale_ref, shift_ref, h_ref):
    x = x_ref[...]
    y = x * lax.rsqrt(jnp.mean(x * x, axis=-1, keepdims=True) + NORM_EPS)
    h = (y * g_ref[...]) * (1.0 + scale_ref[...]) + shift_ref[...]
    h_ref[...] = h.astype(BF16)


def _prep(x, g, scale, shift, scale_spec, shift_spec, tm):
    m, d = x.shape
    return pl.pallas_call(
        _prep_kernel,
        grid=(m // tm,),
        in_specs=[pl.BlockSpec((tm, d), lambda i: (i, 0)),
                  pl.BlockSpec((1, d), lambda i: (0, 0)),
                  scale_spec, shift_spec],
        out_specs=pl.BlockSpec((tm, d), lambda i: (i, 0)),
        out_shape=jax.ShapeDtypeStruct((m, d), BF16),
        compiler_params=_params("arbitrary"),
        name="prep",
    )(x, g, scale, shift)


def _proj_kernel(hp_ref, hs_ref, w_ref, op_ref, os_ref, wb_ref):
    @pl.when(pl.program_id(1) == 0)
    def _():
        wb_ref[...] = w_ref[...].astype(BF16)
        os_ref[...] = jnp.dot(hs_ref[...], wb_ref[...], preferred_element_type=F32).astype(os_ref.dtype)

    op_ref[...] = jnp.dot(hp_ref[...], wb_ref[...], preferred_element_type=F32).astype(op_ref.dtype)


def _proj(h_p, h_s, w, col0, ncol, dtype, tm=1024, tn=1024):
    mp, d = h_p.shape
    ms = h_s.shape[0]
    return pl.pallas_call(
        _proj_kernel,
        grid=(ncol, mp // tm),
        in_specs=[pl.BlockSpec((tm, d), lambda j, i: (i, 0)),
                  pl.BlockSpec((ms, d), lambda j, i: (0, 0)),
                  pl.BlockSpec((d, tn), lambda j, i: (0, col0 + j))],
        out_specs=[pl.BlockSpec((None, tm, tn), lambda j, i: (j, i, 0)),
                   pl.BlockSpec((None, ms, tn), lambda j, i: (j, 0, 0))],
        out_shape=[jax.ShapeDtypeStruct((ncol, mp, tn), dtype),
                   jax.ShapeDtypeStruct((ncol, ms, tn), dtype)],
        scratch_shapes=[pltpu.VMEM((d, tn), BF16)],
        compiler_params=_params("arbitrary", "arbitrary"),
        name="proj",
    )(h_p, h_s, w)


def _topk_mask(gate, valid, lane_f, n_lanes):
    g = jnp.where(valid, gate, -jnp.inf)
    sel = jnp.zeros(gate.shape, F32)
    for _ in range(MOBA_TOPK):
        m = jnp.max(g, axis=1, keepdims=True)
        first = jnp.min(jnp.where(g == m, lane_f, float(n_lanes)), axis=1, keepdims=True)
        hit = lane_f == first
        sel = jnp.where(hit & valid, 1.0, sel)
        g = jnp.where(hit, -jnp.inf, g)
    return sel


def _attn_p_kernel(slopes_ref, q_ref, k_ref, v_ref, o_ref, kb_ref, vb_ref, kmean_ref, *, n_blocks):
    blk = MOBA_BLOCK
    h = pl.program_id(1)
    qi = pl.program_id(2)

    @pl.when(qi == 0)
    def _():
        for n in range(n_blocks):
            rows = slice(n * blk, (n + 1) * blk)
            kf = k_ref[rows, :]
            kb_ref[rows, :] = kf.astype(BF16)
            vb_ref[rows, :] = v_ref[rows, :].astype(BF16)
            kmean_ref[n:n + 1, :] = jnp.sum(kf, axis=0, keepdims=True) * (1.0 / blk)

    slope = slopes_ref[h]
    qf = q_ref[...]
    qb = qf.astype(BF16)

    gate = lax.dot_general(qf, kmean_ref[...], _NT, precision=lax.Precision.HIGHEST,
                           preferred_element_type=F32)
    lane = lax.broadcasted_iota(jnp.int32, gate.shape, 1)
    lane_f = lane.astype(F32)
    sel = _topk_mask(gate, lane < qi, lane_f, n_blocks)

    rel = (lax.broadcasted_iota(jnp.int32, (blk, blk), 0)
           - lax.broadcasted_iota(jnp.int32, (blk, blk), 1))
    rel_f = rel.astype(F32)

    def scores(n, dist_f):
        start = pl.multiple_of(n * blk, blk)
        kn = kb_ref[pl.ds(start, blk), :]
        vn = vb_ref[pl.ds(start, blk), :]
        s = lax.dot_general(qb, kn, _NT, preferred_element_type=F32) * ATTN_SCALE
        return s - slope * dist_f, vn

    s, vn = scores(qi, rel_f)
    s = jnp.where(rel >= 0, s, NEG)
    m0 = jnp.max(s, axis=1, keepdims=True)
    p = jnp.exp(s - m0)
    l0 = jnp.sum(p, axis=1, keepdims=True)
    acc0 = jnp.dot(p.astype(BF16), vn, preferred_element_type=F32)

    def body(n, carry):
        m, l, acc = carry
        off = ((qi - n) * blk).astype(F32)
        s, vn = scores(n, rel_f + off)
        picked = jnp.sum(jnp.where(lane == n, sel, 0.0), axis=1, keepdims=True) > 0.5
        s = jnp.where(picked, s, NEG)
        m_new = jnp.maximum(m, jnp.max(s, axis=1, keepdims=True))
        alpha = jnp.exp(m - m_new)
        p = jnp.exp(s - m_new)
        l = alpha * l + jnp.sum(p, axis=1, keepdims=True)
        acc = alpha * acc + jnp.dot(p.astype(BF16), vn, preferred_element_type=F32)
        return m_new, l, acc

    _, l, acc = lax.fori_loop(0, qi, body, (m0, l0, acc0))
    o_ref[...] = (acc / l).astype(o_ref.dtype)


def _attn_p(q, k, v, slopes, batch, seq):
    n_blocks = seq // MOBA_BLOCK
    blk = MOBA_BLOCK
    kv_spec = pl.BlockSpec((seq, HEAD_DIM), lambda b, h, qi: (b, h))
    q_spec = pl.BlockSpec((blk, HEAD_DIM), lambda b, h, qi: (b * n_blocks + qi, h))
    return pl.pallas_call(
        functools.partial(_attn_p_kernel, n_blocks=n_blocks),
        grid=(batch, N_HEADS, n_blocks),
        in_specs=[pl.BlockSpec(memory_space=pltpu.SMEM), q_spec, kv_spec, kv_spec],
        out_specs=q_spec,
        out_shape=jax.ShapeDtypeStruct(q.shape, BF16),
        scratch_shapes=[pltpu.VMEM((seq, HEAD_DIM), BF16),
                        pltpu.VMEM((seq, HEAD_DIM), BF16),
                        pltpu.VMEM((n_blocks, HEAD_DIM), F32)],
        compiler_params=_params("arbitrary", "arbitrary", "arbitrary"),
        name="attn_p",
    )(slopes, q, k, v)


PAGES_PER_STEP = 8


def _page_specs(page_size, width, n_pages):
    def spec(j):
        return pl.BlockSpec((None, page_size, width),
                            lambda b, c, pt: (pt[b * n_pages + c * PAGES_PER_STEP + j], 0, 0))
    return [spec(j) for j in range(PAGES_PER_STEP)]


def _kstream_kernel(pt_ref, qrows_ref, *refs, page_size):
    k_refs = refs[:PAGES_PER_STEP]
    st_ref, ksum_ref, kb_ref = refs[PAGES_PER_STEP:]
    prev = None
    for j in range(PAGES_PER_STEP):
        kf = k_refs[j][...]
        kb_ref[j * page_size:(j + 1) * page_size, :] = kf.astype(BF16)
        ks = jnp.sum(kf, axis=0, keepdims=True)
        if j % 2 == 0:
            prev = ks
        else:
            ksum_ref[j // 2:j // 2 + 1, :] = prev + ks
    st_ref[...] = lax.dot_general(qrows_ref[...].astype(BF16), kb_ref[...], _NT,
                                  preferred_element_type=F32)


def _kstream(pt, qrows, cache_k, n_pages):
    db, rows, width = qrows.shape
    page_size = cache_k.shape[1]
    steps = n_pages // PAGES_PER_STEP
    keys_per_step = PAGES_PER_STEP * page_size
    blocks_per_step = keys_per_step // MOBA_BLOCK
    grid_spec = pltpu.PrefetchScalarGridSpec(
        num_scalar_prefetch=1,
        grid=(db, steps),
        in_specs=[pl.BlockSpec((None, rows, width), lambda b, c, pt: (b, 0, 0))]
        + _page_specs(page_size, width, n_pages),
        out_specs=[pl.BlockSpec((None, rows, keys_per_step), lambda b, c, pt: (b, 0, c)),
                   pl.BlockSpec((None, None, blocks_per_step, width), lambda b, c, pt: (b, c, 0, 0))],
        scratch_shapes=[pltpu.VMEM((keys_per_step, width), BF16)],
    )
    st, ksum = pl.pallas_call(
        functools.partial(_kstream_kernel, page_size=page_size),
        grid_spec=grid_spec,
        out_shape=[jax.ShapeDtypeStruct((db, rows, n_pages * page_size), F32),
                   jax.ShapeDtypeStruct((db, steps, blocks_per_step, width), F32)],
        compiler_params=_params("arbitrary", "arbitrary"),
        name="kstream",
    )(pt, qrows, *([cache_k] * PAGES_PER_STEP))
    return st, ksum.reshape(db, steps * blocks_per_step, width)


def _select_kernel(qrows_ref, ksum_ref, st_ref, knew_ref, srow_ref, p_ref, pown_ref, l_ref,
                   *, n_new):
    rows, past_len = st_ref.shape
    n_blocks = ksum_ref.shape[0]
    qr = qrows_ref[...]
    kmean = ksum_ref[...] * (1.0 / MOBA_BLOCK)
    gate = lax.dot_general(qr, kmean, _NT, precision=lax.Precision.HIGHEST,
                           preferred_element_type=F32)
    lane = lax.broadcasted_iota(jnp.int32, gate.shape, 1)
    sel = _topk_mask(gate, lane >= 0, lane.astype(F32), n_blocks)

    expand = (lax.broadcasted_iota(jnp.int32, (n_blocks, past_len), 1) // MOBA_BLOCK
              == lax.broadcasted_iota(jnp.int32, (n_blocks, past_len), 0)).astype(BF16)
    picked = jnp.dot(sel.astype(BF16), expand, preferred_element_type=F32) > 0.5

    slope = srow_ref[:, 0:1]
    qq = lax.broadcasted_iota(jnp.int32, (rows, 1), 0) // N_HEADS
    k_pos = lax.broadcasted_iota(jnp.int32, (rows, past_len), 1)
    dist = (past_len + qq) - k_pos
    s = st_ref[...] * ATTN_SCALE - slope * dist.astype(F32)
    s = jnp.where(picked, s, NEG)

    qb = qr.astype(BF16)
    j = lax.broadcasted_iota(jnp.int32, (rows, knew_ref.shape[0]), 1)
    so = lax.dot_general(qb, knew_ref[...].astype(BF16), _NT, preferred_element_type=F32) * ATTN_SCALE
    so = so - slope * (qq - j).astype(F32)
    so = jnp.where((j <= qq) & (j < n_new), so, NEG)

    m = jnp.maximum(jnp.max(s, axis=1, keepdims=True), jnp.max(so, axis=1, keepdims=True))
    p = jnp.exp(s - m)
    po = jnp.exp(so - m)
    l = jnp.sum(p, axis=1, keepdims=True) + jnp.sum(po, axis=1, keepdims=True)
    p_ref[...] = p.astype(BF16)
    pown_ref[...] = po.astype(BF16)
    l_ref[...] = jnp.broadcast_to(l, l_ref.shape)


def _select(qrows, ksum, st, knew, srow, n_new):
    db, rows, width = qrows.shape
    n_blocks = ksum.shape[1]
    past_len = st.shape[2]
    pad = knew.shape[1]
    per_b = lambda *shape: pl.BlockSpec((None,) + shape, lambda b: (b,) + (0,) * len(shape))
    return pl.pallas_call(
        functools.partial(_select_kernel, n_new=n_new),
        grid=(db,),
        in_specs=[per_b(rows, width), per_b(n_blocks, width), per_b(rows, past_len),
                  per_b(pad, width), pl.BlockSpec((rows, HEAD_DIM), lambda b: (0, 0))],
        out_specs=[per_b(rows, past_len), per_b(rows, pad), per_b(rows, HEAD_DIM)],
        out_shape=[jax.ShapeDtypeStruct((db, rows, past_len), BF16),
                   jax.ShapeDtypeStruct((db, rows, pad), BF16),
                   jax.ShapeDtypeStruct((db, rows, HEAD_DIM), F32)],
        compiler_params=_params("arbitrary"),
        name="select",
    )(qrows, ksum, st, knew, srow)


def _vstream_kernel(pt_ref, p_ref, pown_ref, l_ref, vnew_ref, *refs, page_size, n_q):
    v_refs = refs[:PAGES_PER_STEP]
    o_ref, vb_ref, acc_ref = refs[PAGES_PER_STEP:]
    c = pl.program_id(1)

    @pl.when(c == 0)
    def _():
        acc_ref[...] = jnp.zeros(acc_ref.shape, F32)

    for j in range(PAGES_PER_STEP):
        vb_ref[j * page_size:(j + 1) * page_size, :] = v_refs[j][...].astype(BF16)
    acc_ref[...] += jnp.dot(p_ref[...], vb_ref[...], preferred_element_type=F32)

    @pl.when(c == pl.num_programs(1) - 1)
    def _():
        tot = acc_ref[...] + jnp.dot(pown_ref[...], vnew_ref[...].astype(BF16),
                                     preferred_element_type=F32)
        o = tot / l_ref[:, 0:1]
        width = o.shape[1]
        own = (lax.broadcasted_iota(jnp.int32, (N_HEADS, width), 1) // HEAD_DIM
               == lax.broadcasted_iota(jnp.int32, (N_HEADS, width), 0))
        for q in range(n_q):
            rows = o[q * N_HEADS:(q + 1) * N_HEADS, :]
            o_ref[q:q + 1, :] = jnp.sum(jnp.where(own, rows, 0.0), axis=0, keepdims=True)


def _vstream(pt, p, pown, l, vnew, cache_v, n_pages, n_q):
    db, rows, past_len = p.shape
    page_size, width = cache_v.shape[1], cache_v.shape[2]
    pad = vnew.shape[1]
    steps = n_pages // PAGES_PER_STEP
    keys_per_step = PAGES_PER_STEP * page_size
    grid_spec = pltpu.PrefetchScalarGridSpec(
        num_scalar_prefetch=1,
        grid=(db, steps),
        in_specs=[pl.BlockSpec((None, rows, keys_per_step), lambda b, c, pt: (b, 0, c)),
                  pl.BlockSpec((None, rows, pad), lambda b, c, pt: (b, 0, 0)),
                  pl.BlockSpec((None, rows, HEAD_DIM), lambda b, c, pt: (b, 0, 0)),
                  pl.BlockSpec((None, pad, width), lambda b, c, pt: (b, 0, 0))]
        + _page_specs(page_size, width, n_pages),
        out_specs=pl.BlockSpec((None, n_q, width), lambda b, c, pt: (b, 0, 0)),
        scratch_shapes=[pltpu.VMEM((keys_per_step, width), BF16),
                        pltpu.VMEM((rows, width), F32)],
    )
    return pl.pallas_call(
        functools.partial(_vstream_kernel, page_size=page_size, n_q=n_q),
        grid_spec=grid_spec,
        out_shape=jax.ShapeDtypeStruct((db, n_q, width), F32),
        compiler_params=_params("arbitrary", "arbitrary"),
        name="vstream",
    )(pt, p, pown, l, vnew, *([cache_v] * PAGES_PER_STEP))


HALO_ROWS = 16


def _out_kernel(attn_ref, za_ref, bg_ref, cg_ref, u_ref, zb_ref, ga0_ref, ga1_ref, gb0_ref, gb1_ref,
                h1_ref, h2_ref, x_ref, gate_ref, cw_ref, wpa_ref, wpb_ref, wo_ref, fg_ref,
                y_ref, cu_ref, *, sample, tiles_per_seq, n_q):
    tm = x_ref.shape[0]
    cu = cg_ref[...].astype(F32) * u_ref[...].astype(F32)
    row = lax.broadcasted_iota(jnp.int32, cu.shape, 0)
    prev1 = pltpu.roll(cu, 1, 0)
    prev2 = pltpu.roll(cu, 2, 0)
    if sample:
        t = row % n_q
        prev1 = jnp.where(t >= 1, prev1, h1_ref[...])
        prev2 = jnp.where(t >= 2, prev2, h2_ref[...])
        cu_ref[...] = cu
    else:
        first = (pl.program_id(0) % tiles_per_seq) == 0
        halo = h1_ref[...].astype(F32) * h2_ref[...].astype(F32)
        halo = jnp.where(first, 0.0, halo)
        before1 = halo[HALO_ROWS - 1:HALO_ROWS, :]
        before2 = halo[HALO_ROWS - 2:HALO_ROWS - 1, :]
        prev1 = jnp.where(row >= 1, prev1, before1)
        prev2 = jnp.where(row >= 2, prev2, jnp.where(row == 0, before2, before1))
        cu_ref[...] = cu[tm - (CONV_WIDTH - 1):, :]

    conv = bg_ref[...].astype(F32) * (cw_ref[0:1, :] * prev2 + cw_ref[1:2, :] * prev1
                                      + cw_ref[2:3, :] * cu)
    a_act = (attn_ref[...].astype(F32) * jax.nn.silu(za_ref[...].astype(F32))).astype(BF16)
    b_act = (conv * jax.nn.silu(zb_ref[...].astype(F32))).astype(BF16)
    ya = jnp.dot(a_act, wpa_ref[...], preferred_element_type=F32)
    yb = jnp.dot(b_act, wpb_ref[...], preferred_element_type=F32)
    ga = jnp.concatenate([ga0_ref[...], ga1_ref[...]], axis=1).astype(F32)
    gb = jnp.concatenate([gb0_ref[...], gb1_ref[...]], axis=1).astype(F32)
    merged = jax.nn.sigmoid(ga) * ya + jax.nn.sigmoid(gb) * yb
    out = jnp.dot(merged.astype(BF16), wo_ref[...], preferred_element_type=F32)
    xn = x_ref[...] + gate_ref[...] * out
    y = xn * lax.rsqrt(jnp.mean(xn * xn, axis=-1, keepdims=True) + NORM_EPS)
    y_ref[...] = y * fg_ref[...]


def _out(attn, rest, h1, h2, x, gate, gate_spec, conv_w, w_pa, w_pb, w_o, final_g,
         *, sample, tm, tiles_per_seq=1, n_q=1):
    m, d = x.shape
    dc = attn.shape[1]
    tiles = m // tm
    slab = lambda s: pl.BlockSpec((None, tm, dc), lambda i: (s, i, 0))
    const = lambda shape: pl.BlockSpec(shape, lambda i: (0,) * len(shape),
                                       pipeline_mode=pl.Buffered(1))
    if sample:
        hist_specs = [pl.BlockSpec((tm, dc), lambda i: (i, 0))] * 2
        cu_spec = pl.BlockSpec((tm, dc), lambda i: (i, 0))
        cu_shape = jax.ShapeDtypeStruct((m, dc), F32)
    else:
        per_tile = tm // HALO_ROWS
        halo = lambda s: pl.BlockSpec((None, HALO_ROWS, dc),
                                      lambda i: (s, jnp.maximum(i * per_tile - 1, 0), 0))
        hist_specs = [halo(2), halo(3)]
        cu_spec = pl.BlockSpec((None, CONV_WIDTH - 1, dc), lambda i: (i, 0, 0))
        cu_shape = jax.ShapeDtypeStruct((tiles, CONV_WIDTH - 1, dc), F32)
    return pl.pallas_call(
        functools.partial(_out_kernel, sample=sample, tiles_per_seq=tiles_per_seq, n_q=n_q),
        grid=(tiles,),
        in_specs=[pl.BlockSpec((tm, dc), lambda i: (i, 0))]
        + [slab(s) for s in (0, 1, 2, 3, 4, 5, 6, 7, 8)]
        + hist_specs
        + [pl.BlockSpec((tm, d), lambda i: (i, 0)), gate_spec,
           const(conv_w.shape), const(w_pa.shape), const(w_pb.shape), const(w_o.shape),
           const(final_g.shape)],
        out_specs=[pl.BlockSpec((tm, d), lambda i: (i, 0)), cu_spec],
        out_shape=[jax.ShapeDtypeStruct((m, d), F32), cu_shape],
        compiler_params=_params("arbitrary"),
        name="out",
    )(attn, *([rest] * 9), h1, h2, x, gate, conv_w, w_pa, w_pb, w_o, final_g)


def kernel(x_prompt, x_sample, cache_k, cache_v, state_conv, page_table, c_prompt, c_sample,
           norm_g, w_ada, b_ada, w_in, conv_w, w_pa, w_pb, w_o, final_g):
    depth = norm_g.shape[0]
    assert depth == 1, "single-layer step only"
    batch, seq, d = x_prompt.shape
    db, n_q, _ = x_sample.shape
    n_pool, page_size, n_heads, head_dim = cache_k.shape[1:]
    assert (n_heads, head_dim) == (N_HEADS, HEAD_DIM)
    n_pages = page_table.shape[1]
    past_len = n_pages * page_size
    d_attn = n_heads * head_dim
    d_conv = d - d_attn
    assert d_attn == d_conv == 1024 and w_in.shape[2] == 12 * 1024
    assert seq % MOBA_BLOCK == 0 and past_len % MOBA_BLOCK == 0 and n_q <= MOBA_BLOCK

    slopes = jnp.exp2(-8.0 * jnp.arange(1, N_HEADS + 1, dtype=F32) / N_HEADS)

    n_c = batch + db
    c_all = jnp.pad(jnp.concatenate([c_prompt, c_sample], axis=0), ((0, -n_c % 8), (0, 0)))
    mod = _ada(c_all, w_ada[0], b_ada[0][None, :])
    mod4 = mod.reshape(mod.shape[0], 3, 1, d)
    mod_tok = jnp.repeat(mod[batch:n_c].reshape(db, 3, d), n_q, axis=0)

    xp = x_prompt.reshape(batch * seq, d)
    xs = x_sample.reshape(db * n_q, d)
    g = norm_g[0][None, :]
    tm_prep = 512
    per_seq = seq // tm_prep
    mod_spec = lambda s: pl.BlockSpec((None, None, 1, d), lambda i: (i // per_seq, s, 0, 0))
    h_p = _prep(xp, g, mod4, mod4, mod_spec(1), mod_spec(0), tm_prep)
    tok_spec = pl.BlockSpec((db * n_q, d), lambda i: (0, 0))
    h_s = _prep(xs, g, mod_tok[:, 1], mod_tok[:, 0], tok_spec, tok_spec, db * n_q)

    w = w_in[0]
    (q_p,), (q_s,) = _proj(h_p, h_s, w, 0, 1, F32)
    (k_p,), (k_s,) = _proj(h_p, h_s, w, 1, 1, F32)
    (v_p,), (v_s,) = _proj(h_p, h_s, w, 2, 1, F32)
    rest_p, rest_s = _proj(h_p, h_s, w, 3, 9, BF16)

    attn_p = _attn_p(q_p, k_p, v_p, slopes, batch, seq)

    pt = page_table.reshape(-1).astype(jnp.int32)
    ck = cache_k[0].reshape(n_pool, page_size, d_attn)
    cv = cache_v[0].reshape(n_pool, page_size, d_attn)
    rows = n_q * N_HEADS
    q4 = q_s.reshape(db, n_q, N_HEADS, HEAD_DIM)
    qrows = (q4[:, :, :, None, :] * jnp.eye(N_HEADS, dtype=F32)[None, None, :, :, None]
             ).reshape(db, rows, d_attn)
    pad_new = ((0, 0), (0, HEAD_DIM - n_q), (0, 0))
    knew = jnp.pad(k_s.reshape(db, n_q, d_attn), pad_new)
    vnew = jnp.pad(v_s.reshape(db, n_q, d_attn), pad_new)
    srow = jnp.broadcast_to(jnp.tile(slopes, n_q)[:, None], (rows, HEAD_DIM))
    st, ksum = _kstream(pt, qrows, ck, n_pages)
    p, pown, l = _select(qrows, ksum, st, knew, srow, n_q)
    attn_s = _vstream(pt, p, pown, l, vnew, cv, n_pages, n_q).reshape(db * n_q, d_attn)

    cw = conv_w[0]
    wpa, wpb, wo = w_pa[0].astype(BF16), w_pb[0].astype(BF16), w_o[0].astype(BF16)
    fg = final_g[None, :]
    tm_out = 256
    tiles_per_seq = seq // tm_out
    gate_p_spec = pl.BlockSpec((None, None, 1, d), lambda i: (i // tiles_per_seq, 2, 0, 0))
    y_p, cu_tail = _out(attn_p, rest_p, rest_p, rest_p, xp, mod4, gate_p_spec, cw, wpa, wpb, wo, fg,
                        sample=False, tm=tm_out, tiles_per_seq=tiles_per_seq)

    state = state_conv[0]
    zeros = jnp.zeros((db, n_q - 1, d_conv), F32)
    hist1 = jnp.concatenate([state[:, 1:2], zeros], axis=1).reshape(db * n_q, d_conv)
    hist2 = jnp.concatenate([state, zeros[:, 1:]], axis=1).reshape(db * n_q, d_conv)
    gate_s_spec = pl.BlockSpec((db * n_q, d), lambda i: (0, 0))
    y_s, cu_s = _out(attn_s, rest_s, hist1, hist2, xs, mod_tok[:, 2], gate_s_spec, cw, wpa, wpb, wo, fg,
                     sample=True, tm=db * n_q, n_q=n_q)

    conv_p = cu_tail.reshape(batch, tiles_per_seq, CONV_WIDTH - 1, d_conv)[:, -1]
    conv_s = cu_s.reshape(db, n_q, d_conv)[:, n_q - (CONV_WIDTH - 1):]
    kv_p = (1, batch, seq, N_HEADS, HEAD_DIM)
    kv_s = (1, db, n_q, N_HEADS, HEAD_DIM)
    return (y_p.reshape(batch, seq, d), y_s.reshape(db, n_q, d),
            k_p.reshape(kv_p), v_p.reshape(kv_p), conv_p[None],
            k_s.reshape(kv_s), v_s.reshape(kv_s), conv_s[None])
```

```python
import functools

import jax
import jax.numpy as jnp
from jax import lax
from jax.experimental import pallas as pl
from jax.experimental.pallas import tpu as pltpu

F32 = jnp.float32
BF16 = jnp.bfloat16

N_HEADS = 8
HEAD_DIM = 128
MOBA_BLOCK = 256
MOBA_TOPK = 3
CONV_WIDTH = 3
NORM_EPS = 1e-6
NEG = float(jnp.finfo(jnp.float32).min)
MASKED = -1e30
ATTN_SCALE = HEAD_DIM ** -0.5
LOG2_E = 1.4426950408889634
SCALE2 = ATTN_SCALE * LOG2_E
ATTN_VARIANT_STEP = 4

VMEM_LIMIT_BYTES = 56 * 1024 * 1024

_NT = (((1,), (1,)), ((), ()))


def _params(*semantics):
    return pltpu.CompilerParams(dimension_semantics=semantics, vmem_limit_bytes=VMEM_LIMIT_BYTES)


def _ada_kernel(c_ref, w_ref, b_ref, o_ref):
    o_ref[...] = jnp.dot(c_ref[...].astype(BF16), w_ref[...].astype(BF16),
                         preferred_element_type=F32) + b_ref[...]


def _ada(c_all, w_ada, b_ada):
    rows, d = c_all.shape
    n = w_ada.shape[1]
    tn = 1024
    return pl.pallas_call(
        _ada_kernel,
        grid=(n // tn,),
        in_specs=[pl.BlockSpec((rows, d), lambda j: (0, 0)),
                  pl.BlockSpec((d, tn), lambda j: (0, j)),
                  pl.BlockSpec((1, tn), lambda j: (0, j))],
        out_specs=pl.BlockSpec((rows, tn), lambda j: (0, j)),
        out_shape=jax.ShapeDtypeStruct((rows, n), F32),
        compiler_params=_params("arbitrary"),
        name="ada",
    )(c_all, w_ada, b_ada)


def _prep_kernel(x_ref, g_ref, scale_ref, shift_ref, h_ref):
    x = x_ref[...]
    y = x * lax.rsqrt(jnp.mean(x * x, axis=-1, keepdims=True) + NORM_EPS)
    h = (y * g_ref[...]) * (1.0 + scale_ref[...]) + shift_ref[...]
    h_ref[...] = h.astype(BF16)


def _prep(x, g, scale, shift, scale_spec, shift_spec, tm):
    m, d = x.shape
    return pl.pallas_call(
        _prep_kernel,
        grid=(m // tm,),
        in_specs=[pl.BlockSpec((tm, d), lambda i: (i, 0)),
                  pl.BlockSpec((1, d), lambda i: (0, 0)),
                  scale_spec, shift_spec],
        out_specs=pl.BlockSpec((tm, d), lambda i: (i, 0)),
        out_shape=jax.ShapeDtypeStruct((m, d), BF16),
        compiler_params=_params("arbitrary"),
        name="prep",
    )(x, g, scale, shift)


def _proj_kernel(hp_ref, hs_ref, w_ref, op_ref, os_ref, wb_ref):
    @pl.when(pl.program_id(1) == 0)
    def _():
        wb_ref[...] = w_ref[...].astype(BF16)
        os_ref[...] = jnp.dot(hs_ref[...], wb_ref[...], preferred_element_type=F32).astype(os_ref.dtype)

    op_ref[...] = jnp.dot(hp_ref[...], wb_ref[...], preferred_element_type=F32).astype(op_ref.dtype)


def _proj(h_p, h_s, w, col0, ncol, dtype, tm=1024, tn=1024):
    mp, d = h_p.shape
    ms = h_s.shape[0]
    return pl.pallas_call(
        _proj_kernel,
        grid=(ncol, mp // tm),
        in_specs=[pl.BlockSpec((tm, d), lambda j, i: (i, 0)),
                  pl.BlockSpec((ms, d), lambda j, i: (0, 0)),
                  pl.BlockSpec((d, tn), lambda j, i: (0, col0 + j))],
        out_specs=[pl.BlockSpec((None, tm, tn), lambda j, i: (j, i, 0)),
                   pl.BlockSpec((None, ms, tn), lambda j, i: (j, 0, 0))],
        out_shape=[jax.ShapeDtypeStruct((ncol, mp, tn), dtype),
                   jax.ShapeDtypeStruct((ncol, ms, tn), dtype)],
        scratch_shapes=[pltpu.VMEM((d, tn), BF16)],
        compiler_params=_params("arbitrary", "arbitrary"),
        name="proj",
    )(h_p, h_s, w)


def _topk_mask(gate, valid, idx_f, n, axis):
    g = jnp.where(valid, gate, -jnp.inf)
    sel = jnp.zeros(gate.shape, F32)
    for _ in range(MOBA_TOPK):
        m = jnp.max(g, axis=axis, keepdims=True)
        first = jnp.min(jnp.where(g == m, idx_f, float(n)), axis=axis, keepdims=True)
        hit = idx_f == first
        sel = jnp.where(hit & valid, 1.0, sel)
        g = jnp.where(hit, -jnp.inf, g)
    return sel


def _attn_p_kernel(slopes_ref, q_ref, k_ref, v_ref, o_ref,
                   ka_ref, vb_ref, kmean_ref, eye_ref, qa_ref, s_ref, *, n_blocks):
    blk = MOBA_BLOCK
    half = blk // 2
    h = pl.program_id(1)
    qi = pl.program_id(2)

    @pl.when(qi == 0)
    def _():
        lane = lax.broadcasted_iota(jnp.int32, (blk, HEAD_DIM), 1)
        for n in range(n_blocks):
            rows = slice(n * blk, (n + 1) * blk)
            kf = k_ref[rows, :]
            ka_ref[rows, :HEAD_DIM] = kf.astype(BF16)
            ka_ref[rows, HEAD_DIM:] = jnp.where(lane == n, 1.0, 0.0).astype(BF16)
            vb_ref[rows, :] = v_ref[rows, :].astype(BF16)
            kmean_ref[n:n + 1, :] = jnp.sum(kf, axis=0, keepdims=True) * (1.0 / blk)
        eye_ref[...] = jnp.where(lax.broadcasted_iota(jnp.int32, (blk, blk), 0)
                                 == lax.broadcasted_iota(jnp.int32, (blk, blk), 1),
                                 1.0, 0.0).astype(BF16)

    slope2 = slopes_ref[h] * LOG2_E
    qf = q_ref[...]

    gate_t = lax.dot_general(kmean_ref[...], qf, _NT, precision=lax.Precision.HIGHEST,
                             preferred_element_type=F32)
    row = lax.broadcasted_iota(jnp.int32, gate_t.shape, 0)
    sel_t = _topk_mask(gate_t, row < qi, row.astype(F32), n_blocks, axis=0)
    unpicked_t = jnp.concatenate([jnp.where(sel_t > 0.5, 0.0, MASKED),
                                  jnp.zeros((HEAD_DIM - n_blocks, blk), F32)], axis=0)
    unpicked = lax.dot_general(eye_ref[...], unpicked_t.astype(BF16), _NT,
                               preferred_element_type=F32)
    qa_ref[:, :HEAD_DIM] = qf.astype(BF16)
    qa_ref[:, HEAD_DIM:] = unpicked.astype(BF16)

    def halves(x):
        return x[:, :half], x[:, half:]

    def attend(nb):
        key_col = lax.broadcasted_iota(jnp.int32, (1, blk), 1).astype(F32)
        col_bias = slope2 * key_col
        rel = (lax.broadcasted_iota(jnp.int32, (blk, blk), 0)
               - lax.broadcasted_iota(jnp.int32, (blk, blk), 1))
        own = pl.ds(pl.multiple_of(qi * blk, blk), blk)

        s_own = lax.dot_general(qa_ref[:, :HEAD_DIM], ka_ref[own, :HEAD_DIM], _NT,
                                preferred_element_type=F32) * SCALE2 + col_bias
        s_own = jnp.where(rel >= 0, s_own, NEG)
        lo, hi = halves(s_own)
        mx = jnp.maximum(lo, hi)
        for n in range(nb):
            bias_n = col_bias + slope2 * ((n - qi) * blk).astype(F32)
            s = lax.dot_general(qa_ref[...], ka_ref[n * blk:(n + 1) * blk, :], _NT,
                                preferred_element_type=F32) * SCALE2 + bias_n
            s_ref[n] = s
            lo, hi = halves(s)
            mx = jnp.maximum(mx, jnp.maximum(lo, hi))
        m = jnp.broadcast_to(jnp.max(mx, axis=1, keepdims=True), (blk, half))

        def probs(s):
            lo, hi = halves(s)
            lo, hi = jnp.exp2(lo - m), jnp.exp2(hi - m)
            return lo + hi, jnp.concatenate([lo, hi], axis=1).astype(BF16)

        lsum, p = probs(s_own)
        acc = jnp.dot(p, vb_ref[own, :], preferred_element_type=F32)
        for n in range(nb):
            part, p = probs(s_ref[n])
            lsum = lsum + part
            acc = acc + jnp.dot(p, vb_ref[n * blk:(n + 1) * blk, :], preferred_element_type=F32)
        l = jnp.sum(lsum, axis=1, keepdims=True)
        o_ref[...] = (acc / l).astype(o_ref.dtype)

    for nb in range(ATTN_VARIANT_STEP, n_blocks + 1, ATTN_VARIANT_STEP):
        first_qi = 0 if nb == ATTN_VARIANT_STEP else nb - ATTN_VARIANT_STEP + 1
        pl.when((qi >= first_qi) & (qi <= nb))(functools.partial(attend, nb))


def _attn_p(q, k, v, slopes, batch, seq):
    n_blocks = seq // MOBA_BLOCK
    blk = MOBA_BLOCK
    assert n_blocks % ATTN_VARIANT_STEP == 0 and n_blocks <= HEAD_DIM
    kv_spec = pl.BlockSpec((seq, HEAD_DIM), lambda b, h, qi: (b, h))
    q_spec = pl.BlockSpec((blk, HEAD_DIM), lambda b, h, qi: (b * n_blocks + qi, h))
    return pl.pallas_call(
        functools.partial(_attn_p_kernel, n_blocks=n_blocks),
        grid=(batch, N_HEADS, n_blocks),
        in_specs=[pl.BlockSpec(memory_space=pltpu.SMEM), q_spec, kv_spec, kv_spec],
        out_specs=q_spec,
        out_shape=jax.ShapeDtypeStruct(q.shape, BF16),
        scratch_shapes=[pltpu.VMEM((seq, 2 * HEAD_DIM), BF16),
                        pltpu.VMEM((seq, HEAD_DIM), BF16),
                        pltpu.VMEM((n_blocks, HEAD_DIM), F32),
                        pltpu.VMEM((blk, blk), BF16),
                        pltpu.VMEM((blk, 2 * HEAD_DIM), BF16),
                        pltpu.VMEM((n_blocks, blk, blk), F32)],
        compiler_params=_params("arbitrary", "arbitrary", "arbitrary"),
        name="attn_p",
    )(slopes, q, k, v)


PAGES_PER_STEP = 8


def _page_specs(page_size, n_pages):
    def spec(j):
        return pl.BlockSpec((None, page_size * N_HEADS, HEAD_DIM),
                            lambda b, c, pt: (pt[b * n_pages + c * PAGES_PER_STEP + j], 0, 0))
    return [spec(j) for j in range(PAGES_PER_STEP)]


def _head_rows(page_ref, h, page_size):
    return page_ref[pl.ds(h, page_size, stride=N_HEADS), :]


def _kstream_kernel(pt_ref, qrows_ref, *refs, page_size):
    k_refs = refs[:PAGES_PER_STEP]
    st_ref, ksum_ref, kb_ref = refs[PAGES_PER_STEP:]
    pages_per_block = MOBA_BLOCK // page_size
    for blk in range(PAGES_PER_STEP // pages_per_block):
        for h in range(N_HEADS):
            cols = slice(h * HEAD_DIM, (h + 1) * HEAD_DIM)
            tot = None
            for j in range(blk * pages_per_block, (blk + 1) * pages_per_block):
                kf = _head_rows(k_refs[j], h, page_size)
                kb_ref[j * page_size:(j + 1) * page_size, cols] = kf.astype(BF16)
                ks = jnp.sum(kf, axis=0, keepdims=True)
                tot = ks if tot is None else tot + ks
            ksum_ref[blk:blk + 1, cols] = tot
    st_ref[...] = lax.dot_general(qrows_ref[...].astype(BF16), kb_ref[...], _NT,
                                  preferred_element_type=F32)


def _kstream(pt, qrows, cache_k, n_pages):
    db, rows, width = qrows.shape
    page_size = cache_k.shape[1] // N_HEADS
    steps = n_pages // PAGES_PER_STEP
    keys_per_step = PAGES_PER_STEP * page_size
    blocks_per_step = keys_per_step // MOBA_BLOCK
    grid_spec = pltpu.PrefetchScalarGridSpec(
        num_scalar_prefetch=1,
        grid=(db, steps),
        in_specs=[pl.BlockSpec((None, rows, width), lambda b, c, pt: (b, 0, 0))]
        + _page_specs(page_size, n_pages),
        out_specs=[pl.BlockSpec((None, rows, keys_per_step), lambda b, c, pt: (b, 0, c)),
                   pl.BlockSpec((None, None, blocks_per_step, width), lambda b, c, pt: (b, c, 0, 0))],
        scratch_shapes=[pltpu.VMEM((keys_per_step, width), BF16)],
    )
    st, ksum = pl.pallas_call(
        functools.partial(_kstream_kernel, page_size=page_size),
        grid_spec=grid_spec,
        out_shape=[jax.ShapeDtypeStruct((db, rows, n_pages * page_size), F32),
                   jax.ShapeDtypeStruct((db, steps, blocks_per_step, width), F32)],
        compiler_params=_params("arbitrary", "arbitrary"),
        name="kstream",
    )(pt, qrows, *([cache_k] * PAGES_PER_STEP))
    return st, ksum.reshape(db, steps * blocks_per_step, width)


def _select_kernel(qrows_ref, ksum_ref, st_ref, knew_ref, srow_ref, p_ref, pown_ref, l_ref,
                   *, n_new):
    rows, past_len = st_ref.shape
    n_blocks = ksum_ref.shape[0]
    qr = qrows_ref[...]
    kmean = ksum_ref[...] * (1.0 / MOBA_BLOCK)
    gate = lax.dot_general(qr, kmean, _NT, precision=lax.Precision.HIGHEST,
                           preferred_element_type=F32)
    lane = lax.broadcasted_iota(jnp.int32, gate.shape, 1)
    sel = _topk_mask(gate, lane >= 0, lane.astype(F32), n_blocks, axis=1)

    expand = (lax.broadcasted_iota(jnp.int32, (n_blocks, past_len), 1) // MOBA_BLOCK
              == lax.broadcasted_iota(jnp.int32, (n_blocks, past_len), 0)).astype(BF16)
    picked = jnp.dot(sel.astype(BF16), expand, preferred_element_type=F32) > 0.5

    slope = srow_ref[:, 0:1]
    qq = lax.broadcasted_iota(jnp.int32, (rows, 1), 0) // N_HEADS
    k_pos = lax.broadcasted_iota(jnp.int32, (rows, past_len), 1)
    dist = (past_len + qq) - k_pos
    s = st_ref[...] * ATTN_SCALE - slope * dist.astype(F32)
    s = jnp.where(picked, s, NEG)

    qb = qr.astype(BF16)
    j = lax.broadcasted_iota(jnp.int32, (rows, knew_ref.shape[0]), 1)
    so = lax.dot_general(qb, knew_ref[...].astype(BF16), _NT, preferred_element_type=F32) * ATTN_SCALE
    so = so - slope * (qq - j).astype(F32)
    so = jnp.where((j <= qq) & (j < n_new), so, NEG)

    m = jnp.maximum(jnp.max(s, axis=1, keepdims=True), jnp.max(so, axis=1, keepdims=True))
    p = jnp.exp(s - m)
    po = jnp.exp(so - m)
    l = jnp.sum(p, axis=1, keepdims=True) + jnp.sum(po, axis=1, keepdims=True)
    p_ref[...] = p.astype(BF16)
    pown_ref[...] = po.astype(BF16)
    l_ref[...] = jnp.broadcast_to(l, l_ref.shape)


def _select(qrows, ksum, st, knew, srow, n_new):
    db, rows, width = qrows.shape
    n_blocks = ksum.shape[1]
    past_len = st.shape[2]
    pad = knew.shape[1]
    per_b = lambda *shape: pl.BlockSpec((None,) + shape, lambda b: (b,) + (0,) * len(shape))
    return pl.pallas_call(
        functools.partial(_select_kernel, n_new=n_new),
        grid=(db,),
        in_specs=[per_b(rows, width), per_b(n_blocks, width), per_b(rows, past_len),
                  per_b(pad, width), pl.BlockSpec((rows, HEAD_DIM), lambda b: (0, 0))],
        out_specs=[per_b(rows, past_len), per_b(rows, pad), per_b(rows, HEAD_DIM)],
        out_shape=[jax.ShapeDtypeStruct((db, rows, past_len), BF16),
                   jax.ShapeDtypeStruct((db, rows, pad), BF16),
                   jax.ShapeDtypeStruct((db, rows, HEAD_DIM), F32)],
        compiler_params=_params("arbitrary"),
        name="select",
    )(qrows, ksum, st, knew, srow)


def _vstream_kernel(pt_ref, p_ref, pown_ref, l_ref, vnew_ref, *refs, page_size, n_q):
    v_refs = refs[:PAGES_PER_STEP]
    o_ref, vb_ref, acc_ref = refs[PAGES_PER_STEP:]
    c = pl.program_id(1)

    @pl.when(c == 0)
    def _():
        acc_ref[...] = jnp.zeros(acc_ref.shape, F32)

    for j in range(PAGES_PER_STEP):
        for h in range(N_HEADS):
            vb_ref[j * page_size:(j + 1) * page_size, h * HEAD_DIM:(h + 1) * HEAD_DIM] = (
                _head_rows(v_refs[j], h, page_size).astype(BF16))
    acc_ref[...] += jnp.dot(p_ref[...], vb_ref[...], preferred_element_type=F32)

    @pl.when(c == pl.num_programs(1) - 1)
    def _():
        tot = acc_ref[...] + jnp.dot(pown_ref[...], vnew_ref[...].astype(BF16),
                                     preferred_element_type=F32)
        o = tot / l_ref[:, 0:1]
        width = o.shape[1]
        own = (lax.broadcasted_iota(jnp.int32, (N_HEADS, width), 1) // HEAD_DIM
               == lax.broadcasted_iota(jnp.int32, (N_HEADS, width), 0))
        for q in range(n_q):
            rows = o[q * N_HEADS:(q + 1) * N_HEADS, :]
            o_ref[q:q + 1, :] = jnp.sum(jnp.where(own, rows, 0.0), axis=0, keepdims=True)


def _vstream(pt, p, pown, l, vnew, cache_v, n_pages, n_q):
    db, rows, past_len = p.shape
    page_size = cache_v.shape[1] // N_HEADS
    width = vnew.shape[2]
    pad = vnew.shape[1]
    steps = n_pages // PAGES_PER_STEP
    keys_per_step = PAGES_PER_STEP * page_size
    grid_spec = pltpu.PrefetchScalarGridSpec(
        num_scalar_prefetch=1,
        grid=(db, steps),
        in_specs=[pl.BlockSpec((None, rows, keys_per_step), lambda b, c, pt: (b, 0, c)),
                  pl.BlockSpec((None, rows, pad), lambda b, c, pt: (b, 0, 0)),
                  pl.BlockSpec((None, rows, HEAD_DIM), lambda b, c, pt: (b, 0, 0)),
                  pl.BlockSpec((None, pad, width), lambda b, c, pt: (b, 0, 0))]
        + _page_specs(page_size, n_pages),
        out_specs=pl.BlockSpec((None, n_q, width), lambda b, c, pt: (b, 0, 0)),
        scratch_shapes=[pltpu.VMEM((keys_per_step, width), BF16),
                        pltpu.VMEM((rows, width), F32)],
    )
    return pl.pallas_call(
        functools.partial(_vstream_kernel, page_size=page_size, n_q=n_q),
        grid_spec=grid_spec,
        out_shape=jax.ShapeDtypeStruct((db, n_q, width), F32),
        compiler_params=_params("arbitrary", "arbitrary"),
        name="vstream",
    )(pt, p, pown, l, vnew, *([cache_v] * PAGES_PER_STEP))


HALO_ROWS = 16


def _out_kernel(attn_ref, za_ref, bg_ref, cg_ref, u_ref, zb_ref, ga0_ref, ga1_ref, gb0_ref, gb1_ref,
                h1_ref, h2_ref, x_ref, gate_ref, cw_ref, wpa_ref, wpb_ref, wo_ref, fg_ref,
                y_ref, cu_ref, *, sample, tiles_per_seq, n_q):
    tm = x_ref.shape[0]
    cu = cg_ref[...].astype(F32) * u_ref[...].astype(F32)
    row = lax.broadcasted_iota(jnp.int32, cu.shape, 0)
    prev1 = pltpu.roll(cu, 1, 0)
    prev2 = pltpu.roll(cu, 2, 0)
    if sample:
        t = row % n_q
        prev1 = jnp.where(t >= 1, prev1, h1_ref[...])
        prev2 = jnp.where(t >= 2, prev2, h2_ref[...])
        cu_ref[...] = cu
    else:
        first = (pl.program_id(0) % tiles_per_seq) == 0
        halo = h1_ref[...].astype(F32) * h2_ref[...].astype(F32)
        halo = jnp.where(first, 0.0, halo)
        before1 = halo[HALO_ROWS - 1:HALO_ROWS, :]
        before2 = halo[HALO_ROWS - 2:HALO_ROWS - 1, :]
        prev1 = jnp.where(row >= 1, prev1, before1)
        prev2 = jnp.where(row >= 2, prev2, jnp.where(row == 0, before2, before1))
        cu_ref[...] = cu[tm - (CONV_WIDTH - 1):, :]

    conv = bg_ref[...].astype(F32) * (cw_ref[0:1, :] * prev2 + cw_ref[1:2, :] * prev1
                                      + cw_ref[2:3, :] * cu)
    a_act = (attn_ref[...].astype(F32) * jax.nn.silu(za_ref[...].astype(F32))).astype(BF16)
    b_act = (conv * jax.nn.silu(zb_ref[...].astype(F32))).astype(BF16)
    ya = jnp.dot(a_act, wpa_ref[...], preferred_element_type=F32)
    yb = jnp.dot(b_act, wpb_ref[...], preferred_element_type=F32)
    ga = jnp.concatenate([ga0_ref[...], ga1_ref[...]], axis=1).astype(F32)
    gb = jnp.concatenate([gb0_ref[...], gb1_ref[...]], axis=1).astype(F32)
    merged = jax.nn.sigmoid(ga) * ya + jax.nn.sigmoid(gb) * yb
    out = jnp.dot(merged.astype(BF16), wo_ref[...], preferred_element_type=F32)
    xn = x_ref[...] + gate_ref[...] * out
    y = xn * lax.rsqrt(jnp.mean(xn * xn, axis=-1, keepdims=True) + NORM_EPS)
    y_ref[...] = y * fg_ref[...]


def _out(attn, rest, h1, h2, x, gate, gate_spec, conv_w, w_pa, w_pb, w_o, final_g,
         *, sample, tm, tiles_per_seq=1, n_q=1):
    m, d = x.shape
    dc = attn.shape[1]
    tiles = m // tm
    slab = lambda s: pl.BlockSpec((None, tm, dc), lambda i: (s, i, 0))
    const = lambda shape: pl.BlockSpec(shape, lambda i: (0,) * len(shape),
                                       pipeline_mode=pl.Buffered(1))
    if sample:
        hist_specs = [pl.BlockSpec((tm, dc), lambda i: (i, 0))] * 2
        cu_spec = pl.BlockSpec((tm, dc), lambda i: (i, 0))
        cu_shape = jax.ShapeDtypeStruct((m, dc), F32)
    else:
        per_tile = tm // HALO_ROWS
        halo = lambda s: pl.BlockSpec((None, HALO_ROWS, dc),
                                      lambda i: (s, jnp.maximum(i * per_tile - 1, 0), 0))
        hist_specs = [halo(2), halo(3)]
        cu_spec = pl.BlockSpec((None, CONV_WIDTH - 1, dc), lambda i: (i, 0, 0))
        cu_shape = jax.ShapeDtypeStruct((tiles, CONV_WIDTH - 1, dc), F32)
    return pl.pallas_call(
        functools.partial(_out_kernel, sample=sample, tiles_per_seq=tiles_per_seq, n_q=n_q),
        grid=(tiles,),
        in_specs=[pl.BlockSpec((tm, dc), lambda i: (i, 0))]
        + [slab(s) for s in (0, 1, 2, 3, 4, 5, 6, 7, 8)]
        + hist_specs
        + [pl.BlockSpec((tm, d), lambda i: (i, 0)), gate_spec,
           const(conv_w.shape), const(w_pa.shape), const(w_pb.shape), const(w_o.shape),
           const(final_g.shape)],
        out_specs=[pl.BlockSpec((tm, d), lambda i: (i, 0)), cu_spec],
        out_shape=[jax.ShapeDtypeStruct((m, d), F32), cu_shape],
        compiler_params=_params("arbitrary"),
        name="out",
    )(attn, *([rest] * 9), h1, h2, x, gate, conv_w, w_pa, w_pb, w_o, final_g)


def kernel(x_prompt, x_sample, cache_k, cache_v, state_conv, page_table, c_prompt, c_sample,
           norm_g, w_ada, b_ada, w_in, conv_w, w_pa, w_pb, w_o, final_g):
    depth = norm_g.shape[0]
    assert depth == 1, "single-layer step only"
    batch, seq, d = x_prompt.shape
    db, n_q, _ = x_sample.shape
    n_pool, page_size, n_heads, head_dim = cache_k.shape[1:]
    assert (n_heads, head_dim) == (N_HEADS, HEAD_DIM)
    n_pages = page_table.shape[1]
    past_len = n_pages * page_size
    d_attn = n_heads * head_dim
    d_conv = d - d_attn
    assert d_attn == d_conv == 1024 and w_in.shape[2] == 12 * 1024
    assert seq % MOBA_BLOCK == 0 and past_len % MOBA_BLOCK == 0 and n_q <= MOBA_BLOCK

    slopes = jnp.exp2(-8.0 * jnp.arange(1, N_HEADS + 1, dtype=F32) / N_HEADS)

    n_c = batch + db
    c_all = jnp.pad(jnp.concatenate([c_prompt, c_sample], axis=0), ((0, -n_c % 8), (0, 0)))
    mod = _ada(c_all, w_ada[0], b_ada[0][None, :])
    mod4 = mod.reshape(mod.shape[0], 3, 1, d)
    mod_tok = jnp.repeat(mod[batch:n_c].reshape(db, 3, d), n_q, axis=0)

    xp = x_prompt.reshape(batch * seq, d)
    xs = x_sample.reshape(db * n_q, d)
    g = norm_g[0][None, :]
    tm_prep = 512
    per_seq = seq // tm_prep
    mod_spec = lambda s: pl.BlockSpec((None, None, 1, d), lambda i: (i // per_seq, s, 0, 0))
    h_p = _prep(xp, g, mod4, mod4, mod_spec(1), mod_spec(0), tm_prep)
    tok_spec = pl.BlockSpec((db * n_q, d), lambda i: (0, 0))
    h_s = _prep(xs, g, mod_tok[:, 1], mod_tok[:, 0], tok_spec, tok_spec, db * n_q)

    w = w_in[0]
    (q_p,), (q_s,) = _proj(h_p, h_s, w, 0, 1, F32)
    (k_p,), (k_s,) = _proj(h_p, h_s, w, 1, 1, F32)
    (v_p,), (v_s,) = _proj(h_p, h_s, w, 2, 1, F32)
    rest_p, rest_s = _proj(h_p, h_s, w, 3, 9, BF16)

    attn_p = _attn_p(q_p, k_p, v_p, slopes, batch, seq)

    pt = page_table.reshape(-1).astype(jnp.int32)
    ck = cache_k[0].reshape(n_pool, page_size * N_HEADS, HEAD_DIM)
    cv = cache_v[0].reshape(n_pool, page_size * N_HEADS, HEAD_DIM)
    rows = n_q * N_HEADS
    q4 = q_s.reshape(db, n_q, N_HEADS, HEAD_DIM)
    qrows = (q4[:, :, :, None, :] * jnp.eye(N_HEADS, dtype=F32)[None, None, :, :, None]
             ).reshape(db, rows, d_attn)
    pad_new = ((0, 0), (0, HEAD_DIM - n_q), (0, 0))
    knew = jnp.pad(k_s.reshape(db, n_q, d_attn), pad_new)
    vnew = jnp.pad(v_s.reshape(db, n_q, d_attn), pad_new)
    srow = jnp.broadcast_to(jnp.tile(slopes, n_q)[:, None], (rows, HEAD_DIM))
    st, ksum = _kstream(pt, qrows, ck, n_pages)
    p, pown, l = _select(qrows, ksum, st, knew, srow, n_q)
    attn_s = _vstream(pt, p, pown, l, vnew, cv, n_pages, n_q).reshape(db * n_q, d_attn)

    cw = conv_w[0]
    wpa, wpb, wo = w_pa[0].astype(BF16), w_pb[0].astype(BF16), w_o[0].astype(BF16)
    fg = final_g[None, :]
    tm_out = 256
    tiles_per_seq = seq // tm_out
    gate_p_spec = pl.BlockSpec((None, None, 1, d), lambda i: (i // tiles_per_seq, 2, 0, 0))
    y_p, cu_tail = _out(attn_p, rest_p, rest_p, rest_p, xp, mod4, gate_p_spec, cw, wpa, wpb, wo, fg,
                        sample=False, tm=tm_out, tiles_per_seq=tiles_per_seq)

    state = state_conv[0]
    zeros = jnp.zeros((db, n_q - 1, d_conv), F32)
    hist1 = jnp.concatenate([state[:, 1:2], zeros], axis=1).reshape(db * n_q, d_conv)
    hist2 = jnp.concatenate([state, zeros[:, 1:]], axis=1).reshape(db * n_q, d_conv)
    gate_s_spec = pl.BlockSpec((db * n_q, d), lambda i: (0, 0))
    y_s, cu_s = _out(attn_s, rest_s, hist1, hist2, xs, mod_tok[:, 2], gate_s_spec, cw, wpa, wpb, wo, fg,
                     sample=True, tm=db * n_q, n_q=n_q)

    conv_p = cu_tail.reshape(batch, tiles_per_seq, CONV_WIDTH - 1, d_conv)[:, -1]
    conv_s = cu_s.reshape(db, n_q, d_conv)[:, n_q - (CONV_WIDTH - 1):]
    kv_p = (1, batch, seq, N_HEADS, HEAD_DIM)
    kv_s = (1, db, n_q, N_HEADS, HEAD_DIM)
    return (y_p.reshape(batch, seq, d), y_s.reshape(db, n_q, d),
            k_p.reshape(kv_p), v_p.reshape(kv_p), conv_p[None],
            k_s.reshape(kv_s), v_s.reshape(kv_s), conv_s[None])
```

```python
import functools

import jax
import jax.numpy as jnp
from jax import lax
from jax.experimental import pallas as pl
from jax.experimental.pallas import tpu as pltpu

F32 = jnp.float32
BF16 = jnp.bfloat16

N_HEADS = 8
HEAD_DIM = 128
MOBA_BLOCK = 256
MOBA_TOPK = 3
CONV_WIDTH = 3
NORM_EPS = 1e-6
NEG = float(jnp.finfo(jnp.float32).min)
MASKED = -1e30
ATTN_SCALE = HEAD_DIM ** -0.5
LOG2_E = 1.4426950408889634
SCALE2 = ATTN_SCALE * LOG2_E
ATTN_VARIANT_STEP = 4

VMEM_LIMIT_BYTES = 56 * 1024 * 1024

_NT = (((1,), (1,)), ((), ()))


def _params(*semantics):
    return pltpu.CompilerParams(dimension_semantics=semantics, vmem_limit_bytes=VMEM_LIMIT_BYTES)


def _ada_kernel(c_ref, w_ref, b_ref, o_ref):
    o_ref[...] = jnp.dot(c_ref[...].astype(BF16), w_ref[...].astype(BF16),
                         preferred_element_type=F32) + b_ref[...]


def _ada(c_all, w_ada, b_ada):
    rows, d = c_all.shape
    n = w_ada.shape[1]
    tn = 1024
    return pl.pallas_call(
        _ada_kernel,
        grid=(n // tn,),
        in_specs=[pl.BlockSpec((rows, d), lambda j: (0, 0)),
                  pl.BlockSpec((d, tn), lambda j: (0, j)),
                  pl.BlockSpec((1, tn), lambda j: (0, j))],
        out_specs=pl.BlockSpec((rows, tn), lambda j: (0, j)),
        out_shape=jax.ShapeDtypeStruct((rows, n), F32),
        compiler_params=_params("arbitrary"),
        name="ada",
    )(c_all, w_ada, b_ada)


def _prep_kernel(x_ref, g_ref, scale_ref, shift_ref, h_ref):
    x = x_ref[...]
    y = x * lax.rsqrt(jnp.mean(x * x, axis=-1, keepdims=True) + NORM_EPS)
    h = (y * g_ref[...]) * (1.0 + scale_ref[...]) + shift_ref[...]
    h_ref[...] = h.astype(BF16)


def _prep(x, g, scale, shift, scale_spec, shift_spec, tm):
    m, d = x.shape
    return pl.pallas_call(
        _prep_kernel,
        grid=(m // tm,),
        in_specs=[pl.BlockSpec((tm, d), lambda i: (i, 0)),
                  pl.BlockSpec((1, d), lambda i: (0, 0)),
                  scale_spec, shift_spec],
        out_specs=pl.BlockSpec((tm, d), lambda i: (i, 0)),
        out_shape=jax.ShapeDtypeStruct((m, d), BF16),
        compiler_params=_params("arbitrary"),
        name="prep",
    )(x, g, scale, shift)


def _proj_kernel(hp_ref, hs_ref, w_ref, op_ref, os_ref, wb_ref):
    @pl.when(pl.program_id(1) == 0)
    def _():
        wb_ref[...] = w_ref[...].astype(BF16)
        os_ref[...] = jnp.dot(hs_ref[...], wb_ref[...], preferred_element_type=F32).astype(os_ref.dtype)

    op_ref[...] = jnp.dot(hp_ref[...], wb_ref[...], preferred_element_type=F32).astype(op_ref.dtype)


def _proj(h_p, h_s, w, col0, ncol, dtype, tm=1024, tn=1024):
    mp, d = h_p.shape
    ms = h_s.shape[0]
    return pl.pallas_call(
        _proj_kernel,
        grid=(ncol, mp // tm),
        in_specs=[pl.BlockSpec((tm, d), lambda j, i: (i, 0)),
                  pl.BlockSpec((ms, d), lambda j, i: (0, 0)),
                  pl.BlockSpec((d, tn), lambda j, i: (0, col0 + j))],
        out_specs=[pl.BlockSpec((None, tm, tn), lambda j, i: (j, i, 0)),
                   pl.BlockSpec((None, ms, tn), lambda j, i: (j, 0, 0))],
        out_shape=[jax.ShapeDtypeStruct((ncol, mp, tn), dtype),
                   jax.ShapeDtypeStruct((ncol, ms, tn), dtype)],
        scratch_shapes=[pltpu.VMEM((d, tn), BF16)],
        compiler_params=_params("arbitrary", "arbitrary"),
        name="proj",
    )(h_p, h_s, w)


def _topk_mask(gate, valid, idx_f, n, axis):
    g = jnp.where(valid, gate, -jnp.inf)
    sel = jnp.zeros(gate.shape, F32)
    for _ in range(MOBA_TOPK):
        m = jnp.max(g, axis=axis, keepdims=True)
        first = jnp.min(jnp.where(g == m, idx_f, float(n)), axis=axis, keepdims=True)
        hit = idx_f == first
        sel = jnp.where(hit & valid, 1.0, sel)
        g = jnp.where(hit, -jnp.inf, g)
    return sel


def _attn_p_kernel(slopes_ref, q_ref, k_ref, v_ref, o_ref,
                   ka_ref, vb_ref, kmean_ref, eye_ref, qa_ref, s_ref, *, n_blocks):
    blk = MOBA_BLOCK
    half = blk // 2
    h = pl.program_id(1)
    qi = pl.program_id(2)

    @pl.when(qi == 0)
    def _():
        lane = lax.broadcasted_iota(jnp.int32, (blk, HEAD_DIM), 1)
        for n in range(n_blocks):
            rows = slice(n * blk, (n + 1) * blk)
            kf = k_ref[rows, :]
            ka_ref[rows, :HEAD_DIM] = kf.astype(BF16)
            ka_ref[rows, HEAD_DIM:] = jnp.where(lane == n, 1.0, 0.0).astype(BF16)
            vb_ref[rows, :] = v_ref[rows, :].astype(BF16)
            kmean_ref[n:n + 1, :] = jnp.sum(kf, axis=0, keepdims=True) * (1.0 / blk)
        eye_ref[...] = jnp.where(lax.broadcasted_iota(jnp.int32, (blk, blk), 0)
                                 == lax.broadcasted_iota(jnp.int32, (blk, blk), 1),
                                 1.0, 0.0).astype(BF16)

    slope2 = slopes_ref[h] * LOG2_E
    qf = q_ref[...]

    gate_t = lax.dot_general(kmean_ref[...], qf, _NT, precision=lax.Precision.HIGHEST,
                             preferred_element_type=F32)
    row = lax.broadcasted_iota(jnp.int32, gate_t.shape, 0)
    sel_t = _topk_mask(gate_t, row < qi, row.astype(F32), n_blocks, axis=0)
    unpicked_t = jnp.concatenate([jnp.where(sel_t > 0.5, 0.0, MASKED),
                                  jnp.zeros((HEAD_DIM - n_blocks, blk), F32)], axis=0)
    unpicked = lax.dot_general(eye_ref[...], unpicked_t.astype(BF16), _NT,
                               preferred_element_type=F32)
    qa_ref[:, :HEAD_DIM] = qf.astype(BF16)
    qa_ref[:, HEAD_DIM:] = unpicked.astype(BF16)

    def halves(x):
        return x[:, :half], x[:, half:]

    def attend(nb):
        key_col = lax.broadcasted_iota(jnp.int32, (1, blk), 1).astype(F32)
        col_bias = slope2 * key_col
        rel = (lax.broadcasted_iota(jnp.int32, (blk, blk), 0)
               - lax.broadcasted_iota(jnp.int32, (blk, blk), 1))
        own = pl.ds(pl.multiple_of(qi * blk, blk), blk)

        s_own = lax.dot_general(qa_ref[:, :HEAD_DIM], ka_ref[own, :HEAD_DIM], _NT,
                                preferred_element_type=F32) * SCALE2 + col_bias
        s_own = jnp.where(rel >= 0, s_own, NEG)
        lo, hi = halves(s_own)
        mx = jnp.maximum(lo, hi)
        for n in range(nb):
            bias_n = col_bias + slope2 * ((n - qi) * blk).astype(F32)
            s = lax.dot_general(qa_ref[...], ka_ref[n * blk:(n + 1) * blk, :], _NT,
                                preferred_element_type=F32) * SCALE2 + bias_n
            s_ref[n] = s
            lo, hi = halves(s)
            mx = jnp.maximum(mx, jnp.maximum(lo, hi))
        m = jnp.broadcast_to(jnp.max(mx, axis=1, keepdims=True), (blk, half))

        def probs(s):
            lo, hi = halves(s)
            lo, hi = jnp.exp2(lo - m), jnp.exp2(hi - m)
            return lo + hi, jnp.concatenate([lo, hi], axis=1).astype(BF16)

        lsum, p = probs(s_own)
        acc = jnp.dot(p, vb_ref[own, :], preferred_element_type=F32)
        for n in range(nb):
            part, p = probs(s_ref[n])
            lsum = lsum + part
            acc = acc + jnp.dot(p, vb_ref[n * blk:(n + 1) * blk, :], preferred_element_type=F32)
        l = jnp.sum(lsum, axis=1, keepdims=True)
        o_ref[...] = (acc / l).astype(o_ref.dtype)

    for nb in range(ATTN_VARIANT_STEP, n_blocks + 1, ATTN_VARIANT_STEP):
        first_qi = 0 if nb == ATTN_VARIANT_STEP else nb - ATTN_VARIANT_STEP + 1
        pl.when((qi >= first_qi) & (qi <= nb))(functools.partial(attend, nb))


def _attn_p(q, k, v, slopes, batch, seq):
    n_blocks = seq // MOBA_BLOCK
    blk = MOBA_BLOCK
    assert n_blocks % ATTN_VARIANT_STEP == 0 and n_blocks <= HEAD_DIM
    kv_spec = pl.BlockSpec((seq, HEAD_DIM), lambda b, h, qi: (b, h))
    q_spec = pl.BlockSpec((blk, HEAD_DIM), lambda b, h, qi: (b * n_blocks + qi, h))
    return pl.pallas_call(
        functools.partial(_attn_p_kernel, n_blocks=n_blocks),
        grid=(batch, N_HEADS, n_blocks),
        in_specs=[pl.BlockSpec(memory_space=pltpu.SMEM), q_spec, kv_spec, kv_spec],
        out_specs=q_spec,
        out_shape=jax.ShapeDtypeStruct(q.shape, BF16),
        scratch_shapes=[pltpu.VMEM((seq, 2 * HEAD_DIM), BF16),
                        pltpu.VMEM((seq, HEAD_DIM), BF16),
                        pltpu.VMEM((n_blocks, HEAD_DIM), F32),
                        pltpu.VMEM((blk, blk), BF16),
                        pltpu.VMEM((blk, 2 * HEAD_DIM), BF16),
                        pltpu.VMEM((n_blocks, blk, blk), F32)],
        compiler_params=_params("arbitrary", "arbitrary", "arbitrary"),
        name="attn_p",
    )(slopes, q, k, v)


PAGES_PER_STEP = 16
ROWS_PER_HEAD = 8


def _page_specs(page_size, n_pages):
    def spec(j):
        return pl.BlockSpec((None, page_size * N_HEADS, HEAD_DIM),
                            lambda b, c, pt: (pt[b * n_pages + c * PAGES_PER_STEP + j], 0, 0))
    return [spec(j) for j in range(PAGES_PER_STEP)]


def _head_rows(page_ref, h, page_size):
    return page_ref[pl.ds(h, page_size, stride=N_HEADS), :]


def _kstream_kernel(pt_ref, qrows_ref, *refs, page_size):
    k_refs = refs[:PAGES_PER_STEP]
    st_ref, ksum_ref, kb_ref = refs[PAGES_PER_STEP:]
    pages_per_block = MOBA_BLOCK // page_size
    for blk in range(PAGES_PER_STEP // pages_per_block):
        for h in range(N_HEADS):
            cols = slice(h * HEAD_DIM, (h + 1) * HEAD_DIM)
            tot = None
            for j in range(blk * pages_per_block, (blk + 1) * pages_per_block):
                kf = _head_rows(k_refs[j], h, page_size)
                kb_ref[j * page_size:(j + 1) * page_size, cols] = kf.astype(BF16)
                ks = jnp.sum(kf, axis=0, keepdims=True)
                tot = ks if tot is None else tot + ks
            ksum_ref[blk:blk + 1, cols] = tot
    st_ref[...] = lax.dot_general(qrows_ref[...].astype(BF16), kb_ref[...], _NT,
                                  preferred_element_type=F32)


def _kstream(pt, qrows, cache_k, n_pages):
    db, rows, width = qrows.shape
    page_size = cache_k.shape[1] // N_HEADS
    steps = n_pages // PAGES_PER_STEP
    keys_per_step = PAGES_PER_STEP * page_size
    blocks_per_step = keys_per_step // MOBA_BLOCK
    grid_spec = pltpu.PrefetchScalarGridSpec(
        num_scalar_prefetch=1,
        grid=(db, steps),
        in_specs=[pl.BlockSpec((None, rows, width), lambda b, c, pt: (b, 0, 0))]
        + _page_specs(page_size, n_pages),
        out_specs=[pl.BlockSpec((None, rows, keys_per_step), lambda b, c, pt: (b, 0, c)),
                   pl.BlockSpec((None, None, blocks_per_step, width), lambda b, c, pt: (b, c, 0, 0))],
        scratch_shapes=[pltpu.VMEM((keys_per_step, width), BF16)],
    )
    st, ksum = pl.pallas_call(
        functools.partial(_kstream_kernel, page_size=page_size),
        grid_spec=grid_spec,
        out_shape=[jax.ShapeDtypeStruct((db, rows, n_pages * page_size), F32),
                   jax.ShapeDtypeStruct((db, steps, blocks_per_step, width), F32)],
        compiler_params=_params("arbitrary", "arbitrary"),
        name="kstream",
    )(pt, qrows, *([cache_k] * PAGES_PER_STEP))
    return st, ksum.reshape(db, steps * blocks_per_step, width)


def _select_kernel(qrows_ref, ksum_ref, st_ref, knew_ref, srow_ref,
                   pc_ref, pown_ref, l_ref, ids_ref, *, n_new):
    rows, past_len = st_ref.shape
    n_blocks = ksum_ref.shape[0]
    blk = MOBA_BLOCK
    qr = qrows_ref[...]
    kmean = ksum_ref[...] * (1.0 / blk)
    gate = lax.dot_general(qr, kmean, _NT, precision=lax.Precision.HIGHEST,
                           preferred_element_type=F32)
    lane_f = lax.broadcasted_iota(jnp.int32, gate.shape, 1).astype(F32)
    expand = (lax.broadcasted_iota(jnp.int32, (n_blocks, past_len), 1) // blk
              == lax.broadcasted_iota(jnp.int32, (n_blocks, past_len), 0)).astype(BF16)

    slope = srow_ref[:, 0:1]
    qq = lax.broadcasted_iota(jnp.int32, (rows, 1), 0) % ROWS_PER_HEAD
    q_pos = (past_len + qq).astype(F32)
    col = lax.broadcasted_iota(jnp.int32, (1, blk), 1).astype(F32)
    st = st_ref[...]

    g = gate
    firsts, scores = [], []
    for _ in range(MOBA_TOPK):
        m = jnp.max(g, axis=1, keepdims=True)
        first = jnp.min(jnp.where(g == m, lane_f, float(n_blocks)), axis=1, keepdims=True)
        hit = lane_f == first
        g = jnp.where(hit, -jnp.inf, g)
        keep = jnp.dot(jnp.where(hit, 1.0, 0.0).astype(BF16), expand, preferred_element_type=F32)
        kept = st * keep
        sc = kept[:, :blk]
        for n in range(1, n_blocks):
            sc = sc + kept[:, n * blk:(n + 1) * blk]
        dist = (q_pos - first * float(blk)) - col
        scores.append(sc * ATTN_SCALE - slope * dist)
        firsts.append(first)

    j = lax.broadcasted_iota(jnp.int32, (rows, knew_ref.shape[0]), 1)
    so = lax.dot_general(qr.astype(BF16), knew_ref[...].astype(BF16), _NT,
                         preferred_element_type=F32) * ATTN_SCALE
    so = so - slope * (qq - j).astype(F32)
    so = jnp.where((j <= qq) & (j < n_new), so, NEG)

    m = jnp.max(so, axis=1, keepdims=True)
    for s in scores:
        m = jnp.maximum(m, jnp.max(s, axis=1, keepdims=True))
    po = jnp.exp(so - m)
    l = jnp.sum(po, axis=1, keepdims=True)
    for t, s in enumerate(scores):
        p = jnp.exp(s - m)
        l = l + jnp.sum(p, axis=1, keepdims=True)
        pc_ref[:, t * blk:(t + 1) * blk] = p
    pown_ref[...] = po
    l_ref[...] = jnp.broadcast_to(l, l_ref.shape)
    out_lane = lax.broadcasted_iota(jnp.int32, ids_ref.shape, 1)
    ids = jnp.zeros(ids_ref.shape, F32)
    for t, first in enumerate(firsts):
        ids = jnp.where(out_lane == t, first, ids)
    ids_ref[...] = ids.astype(jnp.int32)


def _select(qrows, ksum, st, knew, srow, n_new):
    db, rows, width = qrows.shape
    n_blocks = ksum.shape[1]
    past_len = st.shape[2]
    pad = knew.shape[1]
    assert n_blocks >= MOBA_TOPK
    per_b = lambda *shape: pl.BlockSpec((None,) + shape, lambda b: (b,) + (0,) * len(shape))
    return pl.pallas_call(
        functools.partial(_select_kernel, n_new=n_new),
        grid=(db,),
        in_specs=[per_b(rows, width), per_b(n_blocks, width), per_b(rows, past_len),
                  per_b(pad, width), pl.BlockSpec((rows, HEAD_DIM), lambda b: (0, 0))],
        out_specs=[per_b(rows, MOBA_TOPK * MOBA_BLOCK), per_b(rows, pad), per_b(rows, HEAD_DIM),
                   per_b(rows, HEAD_DIM)],
        out_shape=[jax.ShapeDtypeStruct((db, rows, MOBA_TOPK * MOBA_BLOCK), F32),
                   jax.ShapeDtypeStruct((db, rows, pad), F32),
                   jax.ShapeDtypeStruct((db, rows, HEAD_DIM), F32),
                   jax.ShapeDtypeStruct((db, rows, HEAD_DIM), jnp.int32)],
        compiler_params=_params("arbitrary"),
        name="select",
    )(qrows, ksum, st, knew, srow)


def _attend_s_kernel(pg_ref, pc_ref, pown_ref, l_ref, vnew_ref, cache_ref, o_ref, v_buf, sem,
                     *, n_q, n_slots, page_size):
    h = pl.program_id(1)
    step = pl.program_id(0) * N_HEADS + h
    n_steps = pl.num_programs(0) * N_HEADS
    cur = step % 2

    def page_copy(at_step, half, s):
        return pltpu.make_async_copy(
            cache_ref.at[pg_ref[at_step * n_slots + s], :, at_step % N_HEADS, :],
            v_buf.at[half, s], sem.at[half])

    @pl.when(step == 0)
    def _():
        for s in range(n_slots):
            page_copy(step, cur, s).start()

    @pl.when(step + 1 < n_steps)
    def _():
        for s in range(n_slots):
            page_copy(step + 1, 1 - cur, s).start()

    head_rows = pl.ds(pl.multiple_of(h * ROWS_PER_HEAD, ROWS_PER_HEAD), ROWS_PER_HEAD)
    pc = pc_ref[head_rows, :]
    row = lax.broadcasted_iota(jnp.int32, (ROWS_PER_HEAD, page_size), 0)
    acc = jnp.dot(pown_ref[head_rows, :].astype(BF16), vnew_ref[...].astype(BF16),
                  preferred_element_type=F32)
    for s in range(n_slots):
        page_copy(step, cur, s).wait()
    slots_per_q = n_slots // n_q
    for q in range(n_q):
        for t in range(slots_per_q):
            p = jnp.where(row == q, pc[:, t * page_size:(t + 1) * page_size], 0.0).astype(BF16)
            v = v_buf[cur, q * slots_per_q + t].astype(BF16)
            acc = acc + jnp.dot(p, v, preferred_element_type=F32)
    o_ref[...] = (acc / l_ref[head_rows, 0:1])[:n_q, :]


def _attend_s(pages, pc, pown, l, vnew, cache_v, n_q):
    db, rows, _ = pc.shape
    page_size = cache_v.shape[1]
    n_slots = n_q * MOBA_TOPK * (MOBA_BLOCK // page_size)
    pad = vnew.shape[1]
    per_b = lambda width: pl.BlockSpec((None, rows, width), lambda b, h, pg: (b, 0, 0))
    grid_spec = pltpu.PrefetchScalarGridSpec(
        num_scalar_prefetch=1,
        grid=(db, N_HEADS),
        in_specs=[per_b(pc.shape[2]), per_b(pad), per_b(HEAD_DIM),
                  pl.BlockSpec((None, pad, HEAD_DIM), lambda b, h, pg: (b, 0, h)),
                  pl.BlockSpec(memory_space=pl.ANY)],
        out_specs=pl.BlockSpec((None, n_q, HEAD_DIM), lambda b, h, pg: (b, 0, h)),
        scratch_shapes=[pltpu.VMEM((2, n_slots, page_size, HEAD_DIM), F32),
                        pltpu.SemaphoreType.DMA((2,))],
    )
    return pl.pallas_call(
        functools.partial(_attend_s_kernel, n_q=n_q, n_slots=n_slots, page_size=page_size),
        grid_spec=grid_spec,
        out_shape=jax.ShapeDtypeStruct((db, n_q, N_HEADS * HEAD_DIM), F32),
        compiler_params=_params("arbitrary", "arbitrary"),
        name="attend_s",
    )(pages, pc, pown, l, vnew, cache_v)


def _sample_attention(q_s, k_s, v_s, cache_k, cache_v, page_table, slopes, db, n_q):
    n_pool, page_size = cache_k.shape[:2]
    n_pages = page_table.shape[1]
    d_attn = N_HEADS * HEAD_DIM
    rows = ROWS_PER_HEAD * N_HEADS
    pt = page_table.astype(jnp.int32)
    ck = cache_k.reshape(n_pool, page_size * N_HEADS, HEAD_DIM)
    q4 = jnp.pad(q_s.reshape(db, n_q, N_HEADS, HEAD_DIM).transpose(0, 2, 1, 3),
                 ((0, 0), (0, 0), (0, ROWS_PER_HEAD - n_q), (0, 0)))
    qrows = (q4[:, :, :, None, :] * jnp.eye(N_HEADS, dtype=F32)[None, :, None, :, None]
             ).reshape(db, rows, d_attn)
    pad_new = ((0, 0), (0, HEAD_DIM - n_q), (0, 0))
    knew = jnp.pad(k_s.reshape(db, n_q, d_attn), pad_new)
    vnew = jnp.pad(v_s.reshape(db, n_q, d_attn), pad_new)
    srow = jnp.broadcast_to(jnp.repeat(slopes, ROWS_PER_HEAD)[:, None], (rows, HEAD_DIM))
    st, ksum = _kstream(pt.reshape(-1), qrows, ck, n_pages)
    pc, pown, l, ids = _select(qrows, ksum, st, knew, srow, n_q)
    ppb = MOBA_BLOCK // page_size
    picks = ids[:, :, :MOBA_TOPK].reshape(db, N_HEADS, ROWS_PER_HEAD, MOBA_TOPK)[:, :, :n_q]
    page_idx = picks[..., None] * ppb + jnp.arange(ppb, dtype=jnp.int32)
    pages = jnp.take_along_axis(pt, page_idx.reshape(db, -1), axis=1)
    return _attend_s(pages.reshape(-1), pc, pown, l, vnew, cache_v, n_q).reshape(db * n_q, d_attn)


HALO_ROWS = 16


def _out_kernel(attn_ref, za_ref, bg_ref, cg_ref, u_ref, zb_ref, ga0_ref, ga1_ref, gb0_ref, gb1_ref,
                h1_ref, h2_ref, x_ref, gate_ref, cw_ref, wpa_ref, wpb_ref, wo_ref, fg_ref,
                y_ref, cu_ref, *, sample, tiles_per_seq, n_q):
    tm = x_ref.shape[0]
    cu = cg_ref[...].astype(F32) * u_ref[...].astype(F32)
    row = lax.broadcasted_iota(jnp.int32, cu.shape, 0)
    prev1 = pltpu.roll(cu, 1, 0)
    prev2 = pltpu.roll(cu, 2, 0)
    if sample:
        t = row % n_q
        prev1 = jnp.where(t >= 1, prev1, h1_ref[...])
        prev2 = jnp.where(t >= 2, prev2, h2_ref[...])
        cu_ref[...] = cu
    else:
        first = (pl.program_id(0) % tiles_per_seq) == 0
        halo = h1_ref[...].astype(F32) * h2_ref[...].astype(F32)
        halo = jnp.where(first, 0.0, halo)
        before1 = halo[HALO_ROWS - 1:HALO_ROWS, :]
        before2 = halo[HALO_ROWS - 2:HALO_ROWS - 1, :]
        prev1 = jnp.where(row >= 1, prev1, before1)
        prev2 = jnp.where(row >= 2, prev2, jnp.where(row == 0, before2, before1))
        cu_ref[...] = cu[tm - (CONV_WIDTH - 1):, :]

    conv = bg_ref[...].astype(F32) * (cw_ref[0:1, :] * prev2 + cw_ref[1:2, :] * prev1
                                      + cw_ref[2:3, :] * cu)
    a_act = (attn_ref[...].astype(F32) * jax.nn.silu(za_ref[...].astype(F32))).astype(BF16)
    b_act = (conv * jax.nn.silu(zb_ref[...].astype(F32))).astype(BF16)
    ya = jnp.dot(a_act, wpa_ref[...], preferred_element_type=F32)
    yb = jnp.dot(b_act, wpb_ref[...], preferred_element_type=F32)
    ga = jnp.concatenate([ga0_ref[...], ga1_ref[...]], axis=1).astype(F32)
    gb = jnp.concatenate([gb0_ref[...], gb1_ref[...]], axis=1).astype(F32)
    merged = jax.nn.sigmoid(ga) * ya + jax.nn.sigmoid(gb) * yb
    out = jnp.dot(merged.astype(BF16), wo_ref[...], preferred_element_type=F32)
    xn = x_ref[...] + gate_ref[...] * out
    y = xn * lax.rsqrt(jnp.mean(xn * xn, axis=-1, keepdims=True) + NORM_EPS)
    y_ref[...] = y * fg_ref[...]


def _out(attn, rest, h1, h2, x, gate, gate_spec, conv_w, w_pa, w_pb, w_o, final_g,
         *, sample, tm, tiles_per_seq=1, n_q=1):
    m, d = x.shape
    dc = attn.shape[1]
    tiles = m // tm
    slab = lambda s: pl.BlockSpec((None, tm, dc), lambda i: (s, i, 0))
    const = lambda shape: pl.BlockSpec(shape, lambda i: (0,) * len(shape),
                                       pipeline_mode=pl.Buffered(1))
    if sample:
        hist_specs = [pl.BlockSpec((tm, dc), lambda i: (i, 0))] * 2
        cu_spec = pl.BlockSpec((tm, dc), lambda i: (i, 0))
        cu_shape = jax.ShapeDtypeStruct((m, dc), F32)
    else:
        per_tile = tm // HALO_ROWS
        halo = lambda s: pl.BlockSpec((None, HALO_ROWS, dc),
                                      lambda i: (s, jnp.maximum(i * per_tile - 1, 0), 0))
        hist_specs = [halo(2), halo(3)]
        cu_spec = pl.BlockSpec((None, CONV_WIDTH - 1, dc), lambda i: (i, 0, 0))
        cu_shape = jax.ShapeDtypeStruct((tiles, CONV_WIDTH - 1, dc), F32)
    return pl.pallas_call(
        functools.partial(_out_kernel, sample=sample, tiles_per_seq=tiles_per_seq, n_q=n_q),
        grid=(tiles,),
        in_specs=[pl.BlockSpec((tm, dc), lambda i: (i, 0))]
        + [slab(s) for s in (0, 1, 2, 3, 4, 5, 6, 7, 8)]
        + hist_specs
        + [pl.BlockSpec((tm, d), lambda i: (i, 0)), gate_spec,
           const(conv_w.shape), const(w_pa.shape), const(w_pb.shape), const(w_o.shape),
           const(final_g.shape)],
        out_specs=[pl.BlockSpec((tm, d), lambda i: (i, 0)), cu_spec],
        out_shape=[jax.ShapeDtypeStruct((m, d), F32), cu_shape],
        compiler_params=_params("arbitrary"),
        name="out",
    )(attn, *([rest] * 9), h1, h2, x, gate, conv_w, w_pa, w_pb, w_o, final_g)


def kernel(x_prompt, x_sample, cache_k, cache_v, state_conv, page_table, c_prompt, c_sample,
           norm_g, w_ada, b_ada, w_in, conv_w, w_pa, w_pb, w_o, final_g):
    depth = norm_g.shape[0]
    assert depth == 1, "single-layer step only"
    batch, seq, d = x_prompt.shape
    db, n_q, _ = x_sample.shape
    n_pool, page_size, n_heads, head_dim = cache_k.shape[1:]
    assert (n_heads, head_dim) == (N_HEADS, HEAD_DIM)
    n_pages = page_table.shape[1]
    past_len = n_pages * page_size
    d_attn = n_heads * head_dim
    d_conv = d - d_attn
    assert d_attn == d_conv == 1024 and w_in.shape[2] == 12 * 1024
    assert seq % MOBA_BLOCK == 0 and past_len % MOBA_BLOCK == 0 and n_q <= MOBA_BLOCK

    slopes = jnp.exp2(-8.0 * jnp.arange(1, N_HEADS + 1, dtype=F32) / N_HEADS)

    n_c = batch + db
    c_all = jnp.pad(jnp.concatenate([c_prompt, c_sample], axis=0), ((0, -n_c % 8), (0, 0)))
    mod = _ada(c_all, w_ada[0], b_ada[0][None, :])
    mod4 = mod.reshape(mod.shape[0], 3, 1, d)
    mod_tok = jnp.repeat(mod[batch:n_c].reshape(db, 3, d), n_q, axis=0)

    xp = x_prompt.reshape(batch * seq, d)
    xs = x_sample.reshape(db * n_q, d)
    g = norm_g[0][None, :]
    tm_prep = 512
    per_seq = seq // tm_prep
    mod_spec = lambda s: pl.BlockSpec((None, None, 1, d), lambda i: (i // per_seq, s, 0, 0))
    h_p = _prep(xp, g, mod4, mod4, mod_spec(1), mod_spec(0), tm_prep)
    tok_spec = pl.BlockSpec((db * n_q, d), lambda i: (0, 0))
    h_s = _prep(xs, g, mod_tok[:, 1], mod_tok[:, 0], tok_spec, tok_spec, db * n_q)

    w = w_in[0]
    (q_p,), (q_s,) = _proj(h_p, h_s, w, 0, 1, F32)
    (k_p,), (k_s,) = _proj(h_p, h_s, w, 1, 1, F32)
    (v_p,), (v_s,) = _proj(h_p, h_s, w, 2, 1, F32)
    rest_p, rest_s = _proj(h_p, h_s, w, 3, 9, BF16)

    attn_p = _attn_p(q_p, k_p, v_p, slopes, batch, seq)

    attn_s = _sample_attention(q_s, k_s, v_s, cache_k[0], cache_v[0], page_table, slopes, db, n_q)

    cw = conv_w[0]
    wpa, wpb, wo = w_pa[0].astype(BF16), w_pb[0].astype(BF16), w_o[0].astype(BF16)
    fg = final_g[None, :]
    tm_out = 256
    tiles_per_seq = seq // tm_out
    gate_p_spec = pl.BlockSpec((None, None, 1, d), lambda i: (i // tiles_per_seq, 2, 0, 0))
    y_p, cu_tail = _out(attn_p, rest_p, rest_p, rest_p, xp, mod4, gate_p_spec, cw, wpa, wpb, wo, fg,
                        sample=False, tm=tm_out, tiles_per_seq=tiles_per_seq)

    state = state_conv[0]
    zeros = jnp.zeros((db, n_q - 1, d_conv), F32)
    hist1 = jnp.concatenate([state[:, 1:2], zeros], axis=1).reshape(db * n_q, d_conv)
    hist2 = jnp.concatenate([state, zeros[:, 1:]], axis=1).reshape(db * n_q, d_conv)
    gate_s_spec = pl.BlockSpec((db * n_q, d), lambda i: (0, 0))
    y_s, cu_s = _out(attn_s, rest_s, hist1, hist2, xs, mod_tok[:, 2], gate_s_spec, cw, wpa, wpb, wo, fg,
                     sample=True, tm=db * n_q, n_q=n_q)

    conv_p = cu_tail.reshape(batch, tiles_per_seq, CONV_WIDTH - 1, d_conv)[:, -1]
    conv_s = cu_s.reshape(db, n_q, d_conv)[:, n_q - (CONV_WIDTH - 1):]
    kv_p = (1, batch, seq, N_HEADS, HEAD_DIM)
    kv_s = (1, db, n_q, N_HEADS, HEAD_DIM)
    return (y_p.reshape(batch, seq, d), y_s.reshape(db, n_q, d),
            k_p.reshape(kv_p), v_p.reshape(kv_p), conv_p[None],
            k_s.reshape(kv_s), v_s.reshape(kv_s), conv_s[None])
```

```python
import functools

import jax
import jax.numpy as jnp
from jax import lax
from jax.experimental import pallas as pl
from jax.experimental.pallas import tpu as pltpu

F32 = jnp.float32
BF16 = jnp.bfloat16

N_HEADS = 8
HEAD_DIM = 128
MOBA_BLOCK = 256
MOBA_TOPK = 3
CONV_WIDTH = 3
NORM_EPS = 1e-6
NEG = float(jnp.finfo(jnp.float32).min)
MASKED = -1e30
ATTN_SCALE = HEAD_DIM ** -0.5
LOG2_E = 1.4426950408889634
SCALE2 = ATTN_SCALE * LOG2_E
ATTN_VARIANT_STEP = 2

VMEM_LIMIT_BYTES = 56 * 1024 * 1024

_NT = (((1,), (1,)), ((), ()))


def _params(*semantics):
    return pltpu.CompilerParams(dimension_semantics=semantics, vmem_limit_bytes=VMEM_LIMIT_BYTES)


def _ada_kernel(c_ref, w_ref, b_ref, o_ref):
    o_ref[...] = jnp.dot(c_ref[...].astype(BF16), w_ref[...].astype(BF16),
                         preferred_element_type=F32) + b_ref[...]


def _ada(c_all, w_ada, b_ada):
    rows, d = c_all.shape
    n = w_ada.shape[1]
    tn = 1024
    return pl.pallas_call(
        _ada_kernel,
        grid=(n // tn,),
        in_specs=[pl.BlockSpec((rows, d), lambda j: (0, 0)),
                  pl.BlockSpec((d, tn), lambda j: (0, j)),
                  pl.BlockSpec((1, tn), lambda j: (0, j))],
        out_specs=pl.BlockSpec((rows, tn), lambda j: (0, j)),
        out_shape=jax.ShapeDtypeStruct((rows, n), F32),
        compiler_params=_params("arbitrary"),
        name="ada",
    )(c_all, w_ada, b_ada)


def _prep_kernel(x_ref, g_ref, scale_ref, shift_ref, h_ref):
    x = x_ref[...]
    y = x * lax.rsqrt(jnp.mean(x * x, axis=-1, keepdims=True) + NORM_EPS)
    h = (y * g_ref[...]) * (1.0 + scale_ref[...]) + shift_ref[...]
    h_ref[...] = h.astype(BF16)


def _prep(x, g, scale, shift, scale_spec, shift_spec, tm):
    m, d = x.shape
    return pl.pallas_call(
        _prep_kernel,
        grid=(m // tm,),
        in_specs=[pl.BlockSpec((tm, d), lambda i: (i, 0)),
                  pl.BlockSpec((1, d), lambda i: (0, 0)),
                  scale_spec, shift_spec],
        out_specs=pl.BlockSpec((tm, d), lambda i: (i, 0)),
        out_shape=jax.ShapeDtypeStruct((m, d), BF16),
        compiler_params=_params("arbitrary"),
        name="prep",
    )(x, g, scale, shift)


def _proj_kernel(hp_ref, hs_ref, w_ref, op_ref, os_ref, wb_ref):
    @pl.when(pl.program_id(1) == 0)
    def _():
        wb_ref[...] = w_ref[...].astype(BF16)
        os_ref[...] = jnp.dot(hs_ref[...], wb_ref[...], preferred_element_type=F32).astype(os_ref.dtype)

    op_ref[...] = jnp.dot(hp_ref[...], wb_ref[...], preferred_element_type=F32).astype(op_ref.dtype)


def _proj(h_p, h_s, w, col0, ncol, dtype, tm=1024, tn=1024):
    mp, d = h_p.shape
    ms = h_s.shape[0]
    return pl.pallas_call(
        _proj_kernel,
        grid=(ncol, mp // tm),
        in_specs=[pl.BlockSpec((tm, d), lambda j, i: (i, 0)),
                  pl.BlockSpec((ms, d), lambda j, i: (0, 0)),
                  pl.BlockSpec((d, tn), lambda j, i: (0, col0 + j))],
        out_specs=[pl.BlockSpec((None, tm, tn), lambda j, i: (j, i, 0)),
                   pl.BlockSpec((None, ms, tn), lambda j, i: (j, 0, 0))],
        out_shape=[jax.ShapeDtypeStruct((ncol, mp, tn), dtype),
                   jax.ShapeDtypeStruct((ncol, ms, tn), dtype)],
        scratch_shapes=[pltpu.VMEM((d, tn), BF16)],
        compiler_params=_params("arbitrary", "arbitrary"),
        name="proj",
    )(h_p, h_s, w)


def _topk_mask(gate, valid, idx_f, n, axis):
    g = jnp.where(valid, gate, -jnp.inf)
    sel = jnp.zeros(gate.shape, F32)
    for _ in range(MOBA_TOPK):
        m = jnp.max(g, axis=axis, keepdims=True)
        first = jnp.min(jnp.where(g == m, idx_f, float(n)), axis=axis, keepdims=True)
        hit = idx_f == first
        sel = jnp.where(hit & valid, 1.0, sel)
        g = jnp.where(hit, -jnp.inf, g)
    return sel


ROWS_PER_HEAD = 8


def _head_rows(page_ref, h, page_size):
    return page_ref[pl.ds(h, page_size, stride=N_HEADS), :]


def _kstream_body(qrows_ref, k_refs, st_ref, ksum_ref, kb_ref, page_size):
    pages_per_block = MOBA_BLOCK // page_size
    for blk in range(len(k_refs) // pages_per_block):
        for h in range(N_HEADS):
            cols = slice(h * HEAD_DIM, (h + 1) * HEAD_DIM)
            tot = None
            for j in range(blk * pages_per_block, (blk + 1) * pages_per_block):
                kf = _head_rows(k_refs[j], h, page_size)
                kb_ref[j * page_size:(j + 1) * page_size, cols] = kf.astype(BF16)
                ks = jnp.sum(kf, axis=0, keepdims=True)
                tot = ks if tot is None else tot + ks
            ksum_ref[blk:blk + 1, cols] = tot
    st_ref[...] = lax.dot_general(qrows_ref[...].astype(BF16), kb_ref[...], _NT,
                                  preferred_element_type=F32)


def _attn_p_kernel(pt_ref, slopes_ref, q_ref, k_ref, v_ref, qrows_ref, *refs,
                   n_blocks, n_stream, page_size):
    kp_refs = refs[:n_stream]
    o_ref, st_ref, ksum_ref, ka_ref, vb_ref, kmean_ref, eye_ref, qa_ref, s_ref, kb_ref = refs[n_stream:]
    blk = MOBA_BLOCK
    half = blk // 2
    h = pl.program_id(1)
    qi = pl.program_id(2)

    @pl.when(qi == 0)
    def _():
        lane = lax.broadcasted_iota(jnp.int32, (blk, HEAD_DIM), 1)
        for n in range(n_blocks):
            rows = slice(n * blk, (n + 1) * blk)
            kf = k_ref[rows, :]
            ka_ref[rows, :HEAD_DIM] = kf.astype(BF16)
            ka_ref[rows, HEAD_DIM:] = jnp.where(lane == n, 1.0, 0.0).astype(BF16)
            vb_ref[rows, :] = v_ref[rows, :].astype(BF16)
            kmean_ref[n:n + 1, :] = jnp.sum(kf, axis=0, keepdims=True) * (1.0 / blk)
        eye_ref[...] = jnp.where(lax.broadcasted_iota(jnp.int32, (blk, blk), 0)
                                 == lax.broadcasted_iota(jnp.int32, (blk, blk), 1),
                                 1.0, 0.0).astype(BF16)

    slope2 = slopes_ref[h] * LOG2_E
    qf = q_ref[...]

    gate_t = lax.dot_general(kmean_ref[...], qf, _NT, precision=lax.Precision.HIGHEST,
                             preferred_element_type=F32)
    row = lax.broadcasted_iota(jnp.int32, gate_t.shape, 0)
    sel_t = _topk_mask(gate_t, row < qi, row.astype(F32), n_blocks, axis=0)
    unpicked_t = jnp.concatenate([jnp.where(sel_t > 0.5, 0.0, MASKED),
                                  jnp.zeros((HEAD_DIM - n_blocks, blk), F32)], axis=0)
    unpicked = lax.dot_general(eye_ref[...], unpicked_t.astype(BF16), _NT,
                               preferred_element_type=F32)
    qa_ref[:, :HEAD_DIM] = qf.astype(BF16)
    qa_ref[:, HEAD_DIM:] = unpicked.astype(BF16)

    _kstream_body(qrows_ref, kp_refs, st_ref, ksum_ref, kb_ref, page_size)

    def halves(x):
        return x[:, :half], x[:, half:]

    def attend(nb):
        key_col = lax.broadcasted_iota(jnp.int32, (1, blk), 1).astype(F32)
        col_bias = slope2 * key_col
        rel = (lax.broadcasted_iota(jnp.int32, (blk, blk), 0)
               - lax.broadcasted_iota(jnp.int32, (blk, blk), 1))
        own = pl.ds(pl.multiple_of(qi * blk, blk), blk)

        s_own = lax.dot_general(qa_ref[:, :HEAD_DIM], ka_ref[own, :HEAD_DIM], _NT,
                                preferred_element_type=F32) * SCALE2 + col_bias
        s_own = jnp.where(rel >= 0, s_own, NEG)
        lo, hi = halves(s_own)
        mx = jnp.maximum(lo, hi)
        for n in range(nb):
            bias_n = col_bias + slope2 * ((n - qi) * blk).astype(F32)
            s = lax.dot_general(qa_ref[...], ka_ref[n * blk:(n + 1) * blk, :], _NT,
                                preferred_element_type=F32) * SCALE2 + bias_n
            s_ref[n] = s
            lo, hi = halves(s)
            mx = jnp.maximum(mx, jnp.maximum(lo, hi))
        m = jnp.broadcast_to(jnp.max(mx, axis=1, keepdims=True), (blk, half))

        def probs(s):
            lo, hi = halves(s)
            lo, hi = jnp.exp2(lo - m), jnp.exp2(hi - m)
            return lo + hi, jnp.concatenate([lo, hi], axis=1).astype(BF16)

        lsum, p = probs(s_own)
        acc = jnp.dot(p, vb_ref[own, :], preferred_element_type=F32)
        for n in range(nb):
            part, p = probs(s_ref[n])
            lsum = lsum + part
            acc = acc + jnp.dot(p, vb_ref[n * blk:(n + 1) * blk, :], preferred_element_type=F32)
        l = jnp.sum(lsum, axis=1, keepdims=True)
        o_ref[...] = (acc / l).astype(o_ref.dtype)

    for nb in range(ATTN_VARIANT_STEP, n_blocks + 1, ATTN_VARIANT_STEP):
        first_qi = 0 if nb == ATTN_VARIANT_STEP else nb - ATTN_VARIANT_STEP + 1
        pl.when((qi >= first_qi) & (qi <= nb))(functools.partial(attend, nb))


def _attn_p(q, k, v, slopes, batch, seq, pt, qrows, cache_k, n_pages):
    n_blocks = seq // MOBA_BLOCK
    blk = MOBA_BLOCK
    assert n_blocks % ATTN_VARIANT_STEP == 0 and n_blocks <= HEAD_DIM
    db, rows, width = qrows.shape
    page_size = cache_k.shape[1] // N_HEADS
    n_steps = batch * N_HEADS * n_blocks
    n_stream = db * n_pages // n_steps
    assert n_stream * n_steps == db * n_pages and n_pages % n_stream == 0
    assert (n_stream * page_size) % MOBA_BLOCK == 0
    steps_per_sb = n_pages // n_stream
    keys_per_step = n_stream * page_size
    blocks_per_step = keys_per_step // MOBA_BLOCK

    def stream_pos(b, h, qi):
        lin = (b * N_HEADS + h) * n_blocks + qi
        return lin // steps_per_sb, lin % steps_per_sb

    def page_spec(j):
        def index(b, h, qi, pt):
            sb, c = stream_pos(b, h, qi)
            return pt[sb * n_pages + c * n_stream + j], 0, 0
        return pl.BlockSpec((None, page_size * N_HEADS, HEAD_DIM), index)

    kv_spec = pl.BlockSpec((seq, HEAD_DIM), lambda b, h, qi, pt: (b, h))
    q_spec = pl.BlockSpec((blk, HEAD_DIM), lambda b, h, qi, pt: (b * n_blocks + qi, h))
    grid_spec = pltpu.PrefetchScalarGridSpec(
        num_scalar_prefetch=1,
        grid=(batch, N_HEADS, n_blocks),
        in_specs=[pl.BlockSpec(memory_space=pltpu.SMEM), q_spec, kv_spec, kv_spec,
                  pl.BlockSpec((None, rows, width),
                               lambda b, h, qi, pt: (stream_pos(b, h, qi)[0], 0, 0))]
        + [page_spec(j) for j in range(n_stream)],
        out_specs=[q_spec,
                   pl.BlockSpec((None, rows, keys_per_step),
                                lambda b, h, qi, pt: (stream_pos(b, h, qi)[0], 0, stream_pos(b, h, qi)[1])),
                   pl.BlockSpec((None, None, blocks_per_step, width),
                                lambda b, h, qi, pt: stream_pos(b, h, qi) + (0, 0))],
        scratch_shapes=[pltpu.VMEM((seq, 2 * HEAD_DIM), BF16),
                        pltpu.VMEM((seq, HEAD_DIM), BF16),
                        pltpu.VMEM((n_blocks, HEAD_DIM), F32),
                        pltpu.VMEM((blk, blk), BF16),
                        pltpu.VMEM((blk, 2 * HEAD_DIM), BF16),
                        pltpu.VMEM((n_blocks, blk, blk), F32),
                        pltpu.VMEM((keys_per_step, width), BF16)],
    )
    attn, st, ksum = pl.pallas_call(
        functools.partial(_attn_p_kernel, n_blocks=n_blocks, n_stream=n_stream, page_size=page_size),
        grid_spec=grid_spec,
        out_shape=[jax.ShapeDtypeStruct(q.shape, BF16),
                   jax.ShapeDtypeStruct((db, rows, n_pages * page_size), F32),
                   jax.ShapeDtypeStruct((db, steps_per_sb, blocks_per_step, width), F32)],
        compiler_params=_params("arbitrary", "arbitrary", "arbitrary"),
        name="attn_p",
    )(pt, slopes, q, k, v, qrows, *([cache_k] * n_stream))
    return attn, st, ksum.reshape(db, steps_per_sb * blocks_per_step, width)


def _select_kernel(qrows_ref, ksum_ref, st_ref, knew_ref, srow_ref,
                   pc_ref, pown_ref, l_ref, ids_ref, *, n_new):
    rows, past_len = st_ref.shape
    n_blocks = ksum_ref.shape[0]
    blk = MOBA_BLOCK
    qr = qrows_ref[...]
    kmean = ksum_ref[...] * (1.0 / blk)
    gate = lax.dot_general(qr, kmean, _NT, precision=lax.Precision.HIGHEST,
                           preferred_element_type=F32)
    lane_f = lax.broadcasted_iota(jnp.int32, gate.shape, 1).astype(F32)
    expand = (lax.broadcasted_iota(jnp.int32, (n_blocks, past_len), 1) // blk
              == lax.broadcasted_iota(jnp.int32, (n_blocks, past_len), 0)).astype(BF16)

    slope = srow_ref[:, 0:1]
    qq = lax.broadcasted_iota(jnp.int32, (rows, 1), 0) % ROWS_PER_HEAD
    q_pos = (past_len + qq).astype(F32)
    col = lax.broadcasted_iota(jnp.int32, (1, blk), 1).astype(F32)
    st = st_ref[...]

    g = gate
    firsts, scores = [], []
    for _ in range(MOBA_TOPK):
        m = jnp.max(g, axis=1, keepdims=True)
        first = jnp.min(jnp.where(g == m, lane_f, float(n_blocks)), axis=1, keepdims=True)
        hit = lane_f == first
        g = jnp.where(hit, -jnp.inf, g)
        keep = jnp.dot(jnp.where(hit, 1.0, 0.0).astype(BF16), expand, preferred_element_type=F32)
        kept = st * keep
        sc = kept[:, :blk]
        for n in range(1, n_blocks):
            sc = sc + kept[:, n * blk:(n + 1) * blk]
        dist = (q_pos - first * float(blk)) - col
        scores.append(sc * ATTN_SCALE - slope * dist)
        firsts.append(first)

    j = lax.broadcasted_iota(jnp.int32, (rows, knew_ref.shape[0]), 1)
    so = lax.dot_general(qr.astype(BF16), knew_ref[...].astype(BF16), _NT,
                         preferred_element_type=F32) * ATTN_SCALE
    so = so - slope * (qq - j).astype(F32)
    so = jnp.where((j <= qq) & (j < n_new), so, NEG)

    m = jnp.max(so, axis=1, keepdims=True)
    for s in scores:
        m = jnp.maximum(m, jnp.max(s, axis=1, keepdims=True))
    po = jnp.exp(so - m)
    l = jnp.sum(po, axis=1, keepdims=True)
    for t, s in enumerate(scores):
        p = jnp.exp(s - m)
        l = l + jnp.sum(p, axis=1, keepdims=True)
        pc_ref[:, t * blk:(t + 1) * blk] = p
    pown_ref[...] = po
    l_ref[...] = jnp.broadcast_to(l, l_ref.shape)
    out_lane = lax.broadcasted_iota(jnp.int32, ids_ref.shape, 1)
    ids = jnp.zeros(ids_ref.shape, F32)
    for t, first in enumerate(firsts):
        ids = jnp.where(out_lane == t, first, ids)
    ids_ref[...] = ids.astype(jnp.int32)


def _select(qrows, ksum, st, knew, srow, n_new):
    db, rows, width = qrows.shape
    n_blocks = ksum.shape[1]
    past_len = st.shape[2]
    pad = knew.shape[1]
    assert n_blocks >= MOBA_TOPK
    per_b = lambda *shape: pl.BlockSpec((None,) + shape, lambda b: (b,) + (0,) * len(shape))
    return pl.pallas_call(
        functools.partial(_select_kernel, n_new=n_new),
        grid=(db,),
        in_specs=[per_b(rows, width), per_b(n_blocks, width), per_b(rows, past_len),
                  per_b(pad, width), pl.BlockSpec((rows, HEAD_DIM), lambda b: (0, 0))],
        out_specs=[per_b(rows, MOBA_TOPK * MOBA_BLOCK), per_b(rows, pad), per_b(rows, HEAD_DIM),
                   per_b(rows, HEAD_DIM)],
        out_shape=[jax.ShapeDtypeStruct((db, rows, MOBA_TOPK * MOBA_BLOCK), F32),
                   jax.ShapeDtypeStruct((db, rows, pad), F32),
                   jax.ShapeDtypeStruct((db, rows, HEAD_DIM), F32),
                   jax.ShapeDtypeStruct((db, rows, HEAD_DIM), jnp.int32)],
        compiler_params=_params("arbitrary"),
        name="select",
    )(qrows, ksum, st, knew, srow)


def _attend_s_kernel(picks_ref, pt_ref, pc_ref, pown_ref, l_ref, vnew_ref, cache_ref, o_ref,
                     v_buf, sem, *, n_q, n_slots, n_pages, page_size):
    h = pl.program_id(1)
    step = pl.program_id(0) * N_HEADS + h
    n_steps = pl.num_programs(0) * N_HEADS
    cur = step % 2
    pages_per_block = MOBA_BLOCK // page_size

    def page_copy(at_step, half, s):
        block = picks_ref[at_step * (n_slots // pages_per_block) + s // pages_per_block]
        page = pt_ref[(at_step // N_HEADS) * n_pages + block * pages_per_block + s % pages_per_block]
        return pltpu.make_async_copy(cache_ref.at[page, :, at_step % N_HEADS, :],
                                     v_buf.at[half, s], sem.at[half])

    @pl.when(step == 0)
    def _():
        for s in range(n_slots):
            page_copy(step, cur, s).start()

    @pl.when(step + 1 < n_steps)
    def _():
        for s in range(n_slots):
            page_copy(step + 1, 1 - cur, s).start()

    head_rows = pl.ds(pl.multiple_of(h * ROWS_PER_HEAD, ROWS_PER_HEAD), ROWS_PER_HEAD)
    pc = pc_ref[head_rows, :]
    row = lax.broadcasted_iota(jnp.int32, (ROWS_PER_HEAD, page_size), 0)
    acc = jnp.dot(pown_ref[head_rows, :].astype(BF16), vnew_ref[...].astype(BF16),
                  preferred_element_type=F32)
    for s in range(n_slots):
        page_copy(step, cur, s).wait()
    slots_per_q = n_slots // n_q
    for q in range(n_q):
        for t in range(slots_per_q):
            p = jnp.where(row == q, pc[:, t * page_size:(t + 1) * page_size], 0.0).astype(BF16)
            v = v_buf[cur, q * slots_per_q + t].astype(BF16)
            acc = acc + jnp.dot(p, v, preferred_element_type=F32)
    o_ref[...] = (acc / l_ref[head_rows, 0:1])[:n_q, :]


def _attend_s(picks, pt, pc, pown, l, vnew, cache_v, n_q, n_pages):
    db, rows, _ = pc.shape
    page_size = cache_v.shape[1]
    n_slots = n_q * MOBA_TOPK * (MOBA_BLOCK // page_size)
    pad = vnew.shape[1]
    per_b = lambda width: pl.BlockSpec((None, rows, width), lambda b, h, *_: (b, 0, 0))
    grid_spec = pltpu.PrefetchScalarGridSpec(
        num_scalar_prefetch=2,
        grid=(db, N_HEADS),
        in_specs=[per_b(pc.shape[2]), per_b(pad), per_b(HEAD_DIM),
                  pl.BlockSpec((None, pad, HEAD_DIM), lambda b, h, *_: (b, 0, h)),
                  pl.BlockSpec(memory_space=pl.ANY)],
        out_specs=pl.BlockSpec((None, n_q, HEAD_DIM), lambda b, h, *_: (b, 0, h)),
        scratch_shapes=[pltpu.VMEM((2, n_slots, page_size, HEAD_DIM), F32),
                        pltpu.SemaphoreType.DMA((2,))],
    )
    return pl.pallas_call(
        functools.partial(_attend_s_kernel, n_q=n_q, n_slots=n_slots, n_pages=n_pages,
                          page_size=page_size),
        grid_spec=grid_spec,
        out_shape=jax.ShapeDtypeStruct((db, n_q, N_HEADS * HEAD_DIM), F32),
        compiler_params=_params("arbitrary", "arbitrary"),
        name="attend_s",
    )(picks, pt, pc, pown, l, vnew, cache_v)


def _sample_query_rows(q_s, db, n_q):
    q4 = jnp.pad(q_s.reshape(db, n_q, N_HEADS, HEAD_DIM).transpose(0, 2, 1, 3),
                 ((0, 0), (0, 0), (0, ROWS_PER_HEAD - n_q), (0, 0)))
    return (q4[:, :, :, None, :] * jnp.eye(N_HEADS, dtype=F32)[None, :, None, :, None]
            ).reshape(db, ROWS_PER_HEAD * N_HEADS, N_HEADS * HEAD_DIM)


def _sample_attention(qrows, st, ksum, k_s, v_s, cache_v, pt, slopes, db, n_q, n_pages):
    d_attn = N_HEADS * HEAD_DIM
    rows = ROWS_PER_HEAD * N_HEADS
    pad_new = ((0, 0), (0, HEAD_DIM - n_q), (0, 0))
    knew = jnp.pad(k_s.reshape(db, n_q, d_attn), pad_new)
    vnew = jnp.pad(v_s.reshape(db, n_q, d_attn), pad_new)
    srow = jnp.broadcast_to(jnp.repeat(slopes, ROWS_PER_HEAD)[:, None], (rows, HEAD_DIM))
    pc, pown, l, ids = _select(qrows, ksum, st, knew, srow, n_q)
    picks = ids[:, :, :MOBA_TOPK].reshape(db, N_HEADS, ROWS_PER_HEAD, MOBA_TOPK)[:, :, :n_q]
    return _attend_s(picks.reshape(-1), pt, pc, pown, l, vnew, cache_v, n_q, n_pages
                     ).reshape(db * n_q, d_attn)


HALO_ROWS = 16


def _out_kernel(attn_ref, za_ref, bg_ref, cg_ref, u_ref, zb_ref, ga0_ref, ga1_ref, gb0_ref, gb1_ref,
                h1_ref, h2_ref, x_ref, gate_ref, cw_ref, wpa_ref, wpb_ref, wo_ref, fg_ref,
                y_ref, cu_ref, *, sample, tiles_per_seq, n_q):
    tm = x_ref.shape[0]
    cu = cg_ref[...].astype(F32) * u_ref[...].astype(F32)
    row = lax.broadcasted_iota(jnp.int32, cu.shape, 0)
    prev1 = pltpu.roll(cu, 1, 0)
    prev2 = pltpu.roll(cu, 2, 0)
    if sample:
        t = row % n_q
        prev1 = jnp.where(t >= 1, prev1, h1_ref[...])
        prev2 = jnp.where(t >= 2, prev2, h2_ref[...])
        cu_ref[...] = cu
    else:
        first = (pl.program_id(0) % tiles_per_seq) == 0
        halo = h1_ref[...].astype(F32) * h2_ref[...].astype(F32)
        halo = jnp.where(first, 0.0, halo)
        before1 = halo[HALO_ROWS - 1:HALO_ROWS, :]
        before2 = halo[HALO_ROWS - 2:HALO_ROWS - 1, :]
        prev1 = jnp.where(row >= 1, prev1, before1)
        prev2 = jnp.where(row >= 2, prev2, jnp.where(row == 0, before2, before1))
        cu_ref[...] = cu[tm - (CONV_WIDTH - 1):, :]

    conv = bg_ref[...].astype(F32) * (cw_ref[0:1, :] * prev2 + cw_ref[1:2, :] * prev1
                                      + cw_ref[2:3, :] * cu)
    a_act = (attn_ref[...].astype(F32) * jax.nn.silu(za_ref[...].astype(F32))).astype(BF16)
    b_act = (conv * jax.nn.silu(zb_ref[...].astype(F32))).astype(BF16)
    ya = jnp.dot(a_act, wpa_ref[...], preferred_element_type=F32)
    yb = jnp.dot(b_act, wpb_ref[...], preferred_element_type=F32)
    ga = jnp.concatenate([ga0_ref[...], ga1_ref[...]], axis=1).astype(F32)
    gb = jnp.concatenate([gb0_ref[...], gb1_ref[...]], axis=1).astype(F32)
    merged = jax.nn.sigmoid(ga) * ya + jax.nn.sigmoid(gb) * yb
    out = jnp.dot(merged.astype(BF16), wo_ref[...], preferred_element_type=F32)
    xn = x_ref[...] + gate_ref[...] * out
    y = xn * lax.rsqrt(jnp.mean(xn * xn, axis=-1, keepdims=True) + NORM_EPS)
    y_ref[...] = y * fg_ref[...]


def _out(attn, rest, h1, h2, x, gate, gate_spec, conv_w, w_pa, w_pb, w_o, final_g,
         *, sample, tm, tiles_per_seq=1, n_q=1):
    m, d = x.shape
    dc = attn.shape[1]
    tiles = m // tm
    slab = lambda s: pl.BlockSpec((None, tm, dc), lambda i: (s, i, 0))
    const = lambda shape: pl.BlockSpec(shape, lambda i: (0,) * len(shape),
                                       pipeline_mode=pl.Buffered(1))
    if sample:
        hist_specs = [pl.BlockSpec((tm, dc), lambda i: (i, 0))] * 2
        cu_spec = pl.BlockSpec((tm, dc), lambda i: (i, 0))
        cu_shape = jax.ShapeDtypeStruct((m, dc), F32)
    else:
        per_tile = tm // HALO_ROWS
        halo = lambda s: pl.BlockSpec((None, HALO_ROWS, dc),
                                      lambda i: (s, jnp.maximum(i * per_tile - 1, 0), 0))
        hist_specs = [halo(2), halo(3)]
        cu_spec = pl.BlockSpec((None, CONV_WIDTH - 1, dc), lambda i: (i, 0, 0))
        cu_shape = jax.ShapeDtypeStruct((tiles, CONV_WIDTH - 1, dc), F32)
    return pl.pallas_call(
        functools.partial(_out_kernel, sample=sample, tiles_per_seq=tiles_per_seq, n_q=n_q),
        grid=(tiles,),
        in_specs=[pl.BlockSpec((tm, dc), lambda i: (i, 0))]
        + [slab(s) for s in (0, 1, 2, 3, 4, 5, 6, 7, 8)]
        + hist_specs
        + [pl.BlockSpec((tm, d), lambda i: (i, 0)), gate_spec,
           const(conv_w.shape), const(w_pa.shape), const(w_pb.shape), const(w_o.shape),
           const(final_g.shape)],
        out_specs=[pl.BlockSpec((tm, d), lambda i: (i, 0)), cu_spec],
        out_shape=[jax.ShapeDtypeStruct((m, d), F32), cu_shape],
        compiler_params=_params("arbitrary"),
        name="out",
    )(attn, *([rest] * 9), h1, h2, x, gate, conv_w, w_pa, w_pb, w_o, final_g)


def kernel(x_prompt, x_sample, cache_k, cache_v, state_conv, page_table, c_prompt, c_sample,
           norm_g, w_ada, b_ada, w_in, conv_w, w_pa, w_pb, w_o, final_g):
    depth = norm_g.shape[0]
    assert depth == 1, "single-layer step only"
    batch, seq, d = x_prompt.shape
    db, n_q, _ = x_sample.shape
    n_pool, page_size, n_heads, head_dim = cache_k.shape[1:]
    assert (n_heads, head_dim) == (N_HEADS, HEAD_DIM)
    n_pages = page_table.shape[1]
    past_len = n_pages * page_size
    d_attn = n_heads * head_dim
    d_conv = d - d_attn
    assert d_attn == d_conv == 1024 and w_in.shape[2] == 12 * 1024
    assert seq % MOBA_BLOCK == 0 and past_len % MOBA_BLOCK == 0 and n_q <= MOBA_BLOCK

    slopes = jnp.exp2(-8.0 * jnp.arange(1, N_HEADS + 1, dtype=F32) / N_HEADS)

    n_c = batch + db
    c_all = jnp.pad(jnp.concatenate([c_prompt, c_sample], axis=0), ((0, -n_c % 8), (0, 0)))
    mod = _ada(c_all, w_ada[0], b_ada[0][None, :])
    mod4 = mod.reshape(mod.shape[0], 3, 1, d)
    mod_tok = jnp.repeat(mod[batch:n_c].reshape(db, 3, d), n_q, axis=0)

    xp = x_prompt.reshape(batch * seq, d)
    xs = x_sample.reshape(db * n_q, d)
    g = norm_g[0][None, :]
    tm_prep = 512
    per_seq = seq // tm_prep
    mod_spec = lambda s: pl.BlockSpec((None, None, 1, d), lambda i: (i // per_seq, s, 0, 0))
    h_p = _prep(xp, g, mod4, mod4, mod_spec(1), mod_spec(0), tm_prep)
    tok_spec = pl.BlockSpec((db * n_q, d), lambda i: (0, 0))
    h_s = _prep(xs, g, mod_tok[:, 1], mod_tok[:, 0], tok_spec, tok_spec, db * n_q)

    w = w_in[0]
    (q_p,), (q_s,) = _proj(h_p, h_s, w, 0, 1, F32)
    (k_p,), (k_s,) = _proj(h_p, h_s, w, 1, 1, F32)
    (v_p,), (v_s,) = _proj(h_p, h_s, w, 2, 1, F32)
    rest_p, rest_s = _proj(h_p, h_s, w, 3, 9, BF16)

    ck = cache_k[0].reshape(n_pool, page_size * N_HEADS, HEAD_DIM)
    pt = page_table.reshape(-1).astype(jnp.int32)
    qrows = _sample_query_rows(q_s, db, n_q)
    attn_p, st, ksum = _attn_p(q_p, k_p, v_p, slopes, batch, seq, pt, qrows, ck, n_pages)
    attn_s = _sample_attention(qrows, st, ksum, k_s, v_s, cache_v[0], pt, slopes, db, n_q, n_pages)

    cw = conv_w[0]
    wpa, wpb, wo = w_pa[0].astype(BF16), w_pb[0].astype(BF16), w_o[0].astype(BF16)
    fg = final_g[None, :]
    tm_out = 256
    tiles_per_seq = seq // tm_out
    gate_p_spec = pl.BlockSpec((None, None, 1, d), lambda i: (i // tiles_per_seq, 2, 0, 0))
    y_p, cu_tail = _out(attn_p, rest_p, rest_p, rest_p, xp, mod4, gate_p_spec, cw, wpa, wpb, wo, fg,
                        sample=False, tm=tm_out, tiles_per_seq=tiles_per_seq)

    state = state_conv[0]
    zeros = jnp.zeros((db, n_q - 1, d_conv), F32)
    hist1 = jnp.concatenate([state[:, 1:2], zeros], axis=1).reshape(db * n_q, d_conv)
    hist2 = jnp.concatenate([state, zeros[:, 1:]], axis=1).reshape(db * n_q, d_conv)
    gate_s_spec = pl.BlockSpec((db * n_q, d), lambda i: (0, 0))
    y_s, cu_s = _out(attn_s, rest_s, hist1, hist2, xs, mod_tok[:, 2], gate_s_spec, cw, wpa, wpb, wo, fg,
                     sample=True, tm=db * n_q, n_q=n_q)

    conv_p = cu_tail.reshape(batch, tiles_per_seq, CONV_WIDTH - 1, d_conv)[:, -1]
    conv_s = cu_s.reshape(db, n_q, d_conv)[:, n_q - (CONV_WIDTH - 1):]
    kv_p = (1, batch, seq, N_HEADS, HEAD_DIM)
    kv_s = (1, db, n_q, N_HEADS, HEAD_DIM)
    return (y_p.reshape(batch, seq, d), y_s.reshape(db, n_q, d),
            k_p.reshape(kv_p), v_p.reshape(kv_p), conv_p[None],
            k_s.reshape(kv_s), v_s.reshape(kv_s), conv_s[None])
```

```python
import functools

import jax
import jax.numpy as jnp
from jax import lax
from jax.experimental import pallas as pl
from jax.experimental.pallas import tpu as pltpu

F32 = jnp.float32
BF16 = jnp.bfloat16

N_HEADS = 8
HEAD_DIM = 128
MOBA_BLOCK = 256
MOBA_TOPK = 3
CONV_WIDTH = 3
NORM_EPS = 1e-6
NEG = float(jnp.finfo(jnp.float32).min)
MASKED = -1e30
ATTN_SCALE = HEAD_DIM ** -0.5
LOG2_E = 1.4426950408889634
SCALE2 = ATTN_SCALE * LOG2_E
ATTN_VARIANT_STEP = 2

VMEM_LIMIT_BYTES = 56 * 1024 * 1024

_NT = (((1,), (1,)), ((), ()))


def _params(*semantics):
    return pltpu.CompilerParams(dimension_semantics=semantics, vmem_limit_bytes=VMEM_LIMIT_BYTES)


def _ada_kernel(c_ref, w_ref, b_ref, o_ref):
    o_ref[...] = jnp.dot(c_ref[...].astype(BF16), w_ref[...].astype(BF16),
                         preferred_element_type=F32) + b_ref[...]


def _ada(c_all, w_ada, b_ada):
    rows, d = c_all.shape
    n = w_ada.shape[1]
    tn = 1024
    return pl.pallas_call(
        _ada_kernel,
        grid=(n // tn,),
        in_specs=[pl.BlockSpec((rows, d), lambda j: (0, 0)),
                  pl.BlockSpec((d, tn), lambda j: (0, j)),
                  pl.BlockSpec((1, tn), lambda j: (0, j))],
        out_specs=pl.BlockSpec((rows, tn), lambda j: (0, j)),
        out_shape=jax.ShapeDtypeStruct((rows, n), F32),
        compiler_params=_params("arbitrary"),
        name="ada",
    )(c_all, w_ada, b_ada)


def _prep_kernel(x_ref, g_ref, scale_ref, shift_ref, h_ref):
    x = x_ref[...]
    y = x * lax.rsqrt(jnp.mean(x * x, axis=-1, keepdims=True) + NORM_EPS)
    h = (y * g_ref[...]) * (1.0 + scale_ref[...]) + shift_ref[...]
    h_ref[...] = h.astype(BF16)


def _prep(x, g, scale, shift, scale_spec, shift_spec, tm):
    m, d = x.shape
    return pl.pallas_call(
        _prep_kernel,
        grid=(m // tm,),
        in_specs=[pl.BlockSpec((tm, d), lambda i: (i, 0)),
                  pl.BlockSpec((1, d), lambda i: (0, 0)),
                  scale_spec, shift_spec],
        out_specs=pl.BlockSpec((tm, d), lambda i: (i, 0)),
        out_shape=jax.ShapeDtypeStruct((m, d), BF16),
        compiler_params=_params("arbitrary"),
        name="prep",
    )(x, g, scale, shift)


def _proj_kernel(hp_ref, hs_ref, w_ref, op_ref, os_ref, wb_ref):
    @pl.when(pl.program_id(1) == 0)
    def _():
        wb_ref[...] = w_ref[...].astype(BF16)
        os_ref[...] = jnp.dot(hs_ref[...], wb_ref[...], preferred_element_type=F32).astype(os_ref.dtype)

    op_ref[...] = jnp.dot(hp_ref[...], wb_ref[...], preferred_element_type=F32).astype(op_ref.dtype)


def _proj(h_p, h_s, w, col0, ncol, dtype, tm=1024, tn=1024):
    mp, d = h_p.shape
    ms = h_s.shape[0]
    return pl.pallas_call(
        _proj_kernel,
        grid=(ncol, mp // tm),
        in_specs=[pl.BlockSpec((tm, d), lambda j, i: (i, 0)),
                  pl.BlockSpec((ms, d), lambda j, i: (0, 0)),
                  pl.BlockSpec((d, tn), lambda j, i: (0, col0 + j))],
        out_specs=[pl.BlockSpec((None, tm, tn), lambda j, i: (j, i, 0)),
                   pl.BlockSpec((None, ms, tn), lambda j, i: (j, 0, 0))],
        out_shape=[jax.ShapeDtypeStruct((ncol, mp, tn), dtype),
                   jax.ShapeDtypeStruct((ncol, ms, tn), dtype)],
        scratch_shapes=[pltpu.VMEM((d, tn), BF16)],
        compiler_params=_params("arbitrary", "arbitrary"),
        name="proj",
    )(h_p, h_s, w)


def _topk_mask(gate, valid, idx_f, n, axis):
    g = jnp.where(valid, gate, -jnp.inf)
    sel = jnp.zeros(gate.shape, F32)
    for _ in range(MOBA_TOPK):
        m = jnp.max(g, axis=axis, keepdims=True)
        first = jnp.min(jnp.where(g == m, idx_f, float(n)), axis=axis, keepdims=True)
        hit = idx_f == first
        sel = jnp.where(hit & valid, 1.0, sel)
        g = jnp.where(hit, -jnp.inf, g)
    return sel


SUBLANES = 8
ROWS_PER_HEAD = 4
HEAD_PAIR = N_HEADS // 2
PAIRED = N_HEADS // HEAD_PAIR


def _kstream_repack(k_refs, ksum_ref, kb_ref, page_size):
    pages_per_block = MOBA_BLOCK // page_size
    pair_rows = PAIRED * page_size
    row_hh = lax.broadcasted_iota(jnp.int32, (SUBLANES, HEAD_DIM), 0) % PAIRED
    for blk in range(len(k_refs) // pages_per_block):
        for g in range(HEAD_PAIR):
            tot = None
            for j in range(blk * pages_per_block, (blk + 1) * pages_per_block):
                x = k_refs[j][pl.ds(g, pair_rows, stride=HEAD_PAIR), :]
                kb_ref[j * pair_rows:(j + 1) * pair_rows, g * HEAD_DIM:(g + 1) * HEAD_DIM] = (
                    x.astype(BF16))
                part = jnp.sum(x.reshape(pair_rows // SUBLANES, SUBLANES, HEAD_DIM), axis=0)
                tot = part if tot is None else tot + part
            for hh in range(PAIRED):
                head = g + HEAD_PAIR * hh
                ksum_ref[blk:blk + 1, head * HEAD_DIM:(head + 1) * HEAD_DIM] = jnp.sum(
                    jnp.where(row_hh == hh, tot, 0.0), axis=0, keepdims=True)


def _kstream_scores(qpair_ref, kb_ref, st_ref):
    st_ref[...] = lax.dot_general(qpair_ref[...].astype(BF16), kb_ref[...], _NT,
                                  preferred_element_type=F32)


def _attn_p_kernel(pt_ref, slopes_ref, q_ref, k_ref, v_ref, qrows_ref, *refs,
                   n_blocks, n_stream, page_size):
    kp_refs = refs[:n_stream]
    o_ref, st_ref, ksum_ref, ka_ref, vb_ref, kmean_ref, eye_ref, qa_ref, s_ref, kb_ref = refs[n_stream:]
    blk = MOBA_BLOCK
    half = blk // 2
    h = pl.program_id(1)
    qi = pl.program_id(2)

    @pl.when(qi == 0)
    def _():
        lane = lax.broadcasted_iota(jnp.int32, (blk, HEAD_DIM), 1)
        for n in range(n_blocks):
            rows = slice(n * blk, (n + 1) * blk)
            kf = k_ref[rows, :]
            ka_ref[rows, :HEAD_DIM] = kf.astype(BF16)
            ka_ref[rows, HEAD_DIM:] = jnp.where(lane == n, 1.0, 0.0).astype(BF16)
            vb_ref[rows, :] = v_ref[rows, :].astype(BF16)
            kmean_ref[n:n + 1, :] = jnp.sum(kf, axis=0, keepdims=True) * (1.0 / blk)
        eye_ref[...] = jnp.where(lax.broadcasted_iota(jnp.int32, (blk, blk), 0)
                                 == lax.broadcasted_iota(jnp.int32, (blk, blk), 1),
                                 1.0, 0.0).astype(BF16)

    _kstream_repack(kp_refs, ksum_ref, kb_ref, page_size)

    slope2 = slopes_ref[h] * LOG2_E
    qf = q_ref[...]

    gate_t = lax.dot_general(kmean_ref[...], qf, _NT, precision=lax.Precision.HIGHEST,
                             preferred_element_type=F32)
    row = lax.broadcasted_iota(jnp.int32, gate_t.shape, 0)
    sel_t = _topk_mask(gate_t, row < qi, row.astype(F32), n_blocks, axis=0)
    unpicked_t = jnp.concatenate([jnp.where(sel_t > 0.5, 0.0, MASKED),
                                  jnp.zeros((HEAD_DIM - n_blocks, blk), F32)], axis=0)
    unpicked = lax.dot_general(eye_ref[...], unpicked_t.astype(BF16), _NT,
                               preferred_element_type=F32)
    qa_ref[:, :HEAD_DIM] = qf.astype(BF16)
    qa_ref[:, HEAD_DIM:] = unpicked.astype(BF16)
    _kstream_scores(qrows_ref, kb_ref, st_ref)

    def halves(x):
        return x[:, :half], x[:, half:]

    def attend(nb):
        key_col = lax.broadcasted_iota(jnp.int32, (1, blk), 1).astype(F32)
        col_bias = slope2 * key_col
        rel = (lax.broadcasted_iota(jnp.int32, (blk, blk), 0)
               - lax.broadcasted_iota(jnp.int32, (blk, blk), 1))
        own = pl.ds(pl.multiple_of(qi * blk, blk), blk)

        s_own = lax.dot_general(qa_ref[:, :HEAD_DIM], ka_ref[own, :HEAD_DIM], _NT,
                                preferred_element_type=F32) * SCALE2 + col_bias
        s_own = jnp.where(rel >= 0, s_own, NEG)
        lo, hi = halves(s_own)
        mx = jnp.maximum(lo, hi)
        for n in range(nb):
            bias_n = col_bias + slope2 * ((n - qi) * blk).astype(F32)
            s = lax.dot_general(qa_ref[...], ka_ref[n * blk:(n + 1) * blk, :], _NT,
                                preferred_element_type=F32) * SCALE2 + bias_n
            s_ref[n] = s
            lo, hi = halves(s)
            mx = jnp.maximum(mx, jnp.maximum(lo, hi))
        m = jnp.broadcast_to(jnp.max(mx, axis=1, keepdims=True), (blk, half))

        def probs(s):
            lo, hi = halves(s)
            lo, hi = jnp.exp2(lo - m), jnp.exp2(hi - m)
            return lo + hi, jnp.concatenate([lo, hi], axis=1).astype(BF16)

        lsum, p = probs(s_own)
        acc = jnp.dot(p, vb_ref[own, :], preferred_element_type=F32)
        for n in range(nb):
            part, p = probs(s_ref[n])
            lsum = lsum + part
            acc = acc + jnp.dot(p, vb_ref[n * blk:(n + 1) * blk, :], preferred_element_type=F32)
        l = jnp.sum(lsum, axis=1, keepdims=True)
        o_ref[...] = (acc / l).astype(o_ref.dtype)

    for nb in range(ATTN_VARIANT_STEP, n_blocks + 1, ATTN_VARIANT_STEP):
        first_qi = 0 if nb == ATTN_VARIANT_STEP else nb - ATTN_VARIANT_STEP + 1
        pl.when((qi >= first_qi) & (qi <= nb))(functools.partial(attend, nb))


def _attn_p(q, k, v, slopes, batch, seq, pt, qpair, cache_k, n_pages):
    n_blocks = seq // MOBA_BLOCK
    blk = MOBA_BLOCK
    assert n_blocks % ATTN_VARIANT_STEP == 0 and n_blocks <= HEAD_DIM
    db, rows, pair_width = qpair.shape
    width = N_HEADS * HEAD_DIM
    page_size = cache_k.shape[1] // N_HEADS
    n_steps = batch * N_HEADS * n_blocks
    n_stream = db * n_pages // n_steps
    assert n_stream * n_steps == db * n_pages and n_pages % n_stream == 0
    assert (n_stream * page_size) % MOBA_BLOCK == 0
    steps_per_sb = n_pages // n_stream
    keys_per_step = n_stream * page_size
    blocks_per_step = keys_per_step // MOBA_BLOCK

    def stream_pos(b, h, qi):
        lin = (b * N_HEADS + h) * n_blocks + qi
        return lin // steps_per_sb, lin % steps_per_sb

    def page_spec(j):
        def index(b, h, qi, pt):
            sb, c = stream_pos(b, h, qi)
            return pt[sb * n_pages + c * n_stream + j], 0, 0
        return pl.BlockSpec((None, page_size * N_HEADS, HEAD_DIM), index)

    kv_spec = pl.BlockSpec((seq, HEAD_DIM), lambda b, h, qi, pt: (b, h))
    q_spec = pl.BlockSpec((blk, HEAD_DIM), lambda b, h, qi, pt: (b * n_blocks + qi, h))
    grid_spec = pltpu.PrefetchScalarGridSpec(
        num_scalar_prefetch=1,
        grid=(batch, N_HEADS, n_blocks),
        in_specs=[pl.BlockSpec(memory_space=pltpu.SMEM), q_spec, kv_spec, kv_spec,
                  pl.BlockSpec((None, rows, pair_width),
                               lambda b, h, qi, pt: (stream_pos(b, h, qi)[0], 0, 0))]
        + [page_spec(j) for j in range(n_stream)],
        out_specs=[q_spec,
                   pl.BlockSpec((None, rows, PAIRED * keys_per_step),
                                lambda b, h, qi, pt: (stream_pos(b, h, qi)[0], 0, stream_pos(b, h, qi)[1])),
                   pl.BlockSpec((None, None, blocks_per_step, width),
                                lambda b, h, qi, pt: stream_pos(b, h, qi) + (0, 0))],
        scratch_shapes=[pltpu.VMEM((seq, 2 * HEAD_DIM), BF16),
                        pltpu.VMEM((seq, HEAD_DIM), BF16),
                        pltpu.VMEM((n_blocks, HEAD_DIM), F32),
                        pltpu.VMEM((blk, blk), BF16),
                        pltpu.VMEM((blk, 2 * HEAD_DIM), BF16),
                        pltpu.VMEM((n_blocks, blk, blk), F32),
                        pltpu.VMEM((PAIRED * keys_per_step, pair_width), BF16)],
    )
    attn, st, ksum = pl.pallas_call(
        functools.partial(_attn_p_kernel, n_blocks=n_blocks, n_stream=n_stream, page_size=page_size),
        grid_spec=grid_spec,
        out_shape=[jax.ShapeDtypeStruct(q.shape, BF16),
                   jax.ShapeDtypeStruct((db, rows, PAIRED * n_pages * page_size), F32),
                   jax.ShapeDtypeStruct((db, steps_per_sb, blocks_per_step, width), F32)],
        compiler_params=_params("arbitrary", "arbitrary", "arbitrary"),
        name="attn_p",
    )(pt, slopes, q, k, v, qpair, *([cache_k] * n_stream))
    return attn, st, ksum.reshape(db, steps_per_sb * blocks_per_step, width)


def _dot_onehot_exact(x, onehot):
    hi = x.astype(BF16)
    rest = x - hi.astype(F32)
    mid = rest.astype(BF16)
    lo = (rest - mid.astype(F32)).astype(BF16)
    return ((jnp.dot(hi, onehot, preferred_element_type=F32)
             + jnp.dot(mid, onehot, preferred_element_type=F32))
            + jnp.dot(lo, onehot, preferred_element_type=F32))


def _select_kernel(qrows_ref, ksum_ref, st_ref, knew_ref, srow_ref,
                   pc_ref, pown_ref, l_ref, ids_ref, *, n_new):
    rows = st_ref.shape[0]
    n_blocks = ksum_ref.shape[0]
    blk = MOBA_BLOCK
    past_len = n_blocks * blk
    pblk = PAIRED * blk
    qr = qrows_ref[...]
    kmean = ksum_ref[...] * (1.0 / blk)
    gate = lax.dot_general(qr, kmean, _NT, precision=lax.Precision.HIGHEST,
                           preferred_element_type=F32)
    lane_f = lax.broadcasted_iota(jnp.int32, gate.shape, 1).astype(F32)
    expand = (lax.broadcasted_iota(jnp.int32, (n_blocks, n_blocks * pblk), 1) // pblk
              == lax.broadcasted_iota(jnp.int32, (n_blocks, n_blocks * pblk), 0)).astype(BF16)
    own_col = (lax.broadcasted_iota(jnp.int32, (rows, pblk), 1) % PAIRED
               == lax.broadcasted_iota(jnp.int32, (rows, pblk), 0) // (ROWS_PER_HEAD * HEAD_PAIR))
    compact = (lax.broadcasted_iota(jnp.int32, (pblk, blk), 0) // PAIRED
               == lax.broadcasted_iota(jnp.int32, (pblk, blk), 1)).astype(BF16)

    slope = srow_ref[:, 0:1]
    qq = lax.broadcasted_iota(jnp.int32, (rows, 1), 0) % ROWS_PER_HEAD
    q_pos = (past_len + qq).astype(F32)
    col = lax.broadcasted_iota(jnp.int32, (1, blk), 1).astype(F32)
    st = st_ref[...]

    g = gate
    firsts, scores = [], []
    for _ in range(MOBA_TOPK):
        m = jnp.max(g, axis=1, keepdims=True)
        first = jnp.min(jnp.where(g == m, lane_f, float(n_blocks)), axis=1, keepdims=True)
        hit = lane_f == first
        g = jnp.where(hit, -jnp.inf, g)
        keep = jnp.dot(jnp.where(hit, 1.0, 0.0).astype(BF16), expand, preferred_element_type=F32)
        kept = st * keep
        sc = kept[:, :pblk]
        for n in range(1, n_blocks):
            sc = sc + kept[:, n * pblk:(n + 1) * pblk]
        sc = _dot_onehot_exact(jnp.where(own_col, sc, 0.0), compact)
        dist = (q_pos - first * float(blk)) - col
        scores.append(sc * ATTN_SCALE - slope * dist)
        firsts.append(first)

    j = lax.broadcasted_iota(jnp.int32, (rows, knew_ref.shape[0]), 1)
    so = lax.dot_general(qr.astype(BF16), knew_ref[...].astype(BF16), _NT,
                         preferred_element_type=F32) * ATTN_SCALE
    so = so - slope * (qq - j).astype(F32)
    so = jnp.where((j <= qq) & (j < n_new), so, NEG)

    m = jnp.max(so, axis=1, keepdims=True)
    for s in scores:
        m = jnp.maximum(m, jnp.max(s, axis=1, keepdims=True))
    po = jnp.exp(so - m)
    l = jnp.sum(po, axis=1, keepdims=True)
    for t, s in enumerate(scores):
        p = jnp.exp(s - m)
        l = l + jnp.sum(p, axis=1, keepdims=True)
        pc_ref[:, t * blk:(t + 1) * blk] = p
    pown_ref[...] = po
    l_ref[...] = jnp.broadcast_to(l, l_ref.shape)
    out_lane = lax.broadcasted_iota(jnp.int32, ids_ref.shape, 1)
    ids = jnp.zeros(ids_ref.shape, F32)
    for t, first in enumerate(firsts):
        ids = jnp.where(out_lane == t, first, ids)
    ids_ref[...] = ids.astype(jnp.int32)


def _select(qrows, ksum, st, knew, srow, n_new):
    db, rows, width = qrows.shape
    n_blocks = ksum.shape[1]
    past_len = st.shape[2]
    pad = knew.shape[1]
    assert n_blocks >= MOBA_TOPK and past_len == PAIRED * n_blocks * MOBA_BLOCK
    per_b = lambda *shape: pl.BlockSpec((None,) + shape, lambda b: (b,) + (0,) * len(shape))
    return pl.pallas_call(
        functools.partial(_select_kernel, n_new=n_new),
        grid=(db,),
        in_specs=[per_b(rows, width), per_b(n_blocks, width), per_b(rows, past_len),
                  per_b(pad, width), pl.BlockSpec((rows, HEAD_DIM), lambda b: (0, 0))],
        out_specs=[per_b(rows, MOBA_TOPK * MOBA_BLOCK), per_b(rows, pad), per_b(rows, HEAD_DIM),
                   per_b(rows, HEAD_DIM)],
        out_shape=[jax.ShapeDtypeStruct((db, rows, MOBA_TOPK * MOBA_BLOCK), F32),
                   jax.ShapeDtypeStruct((db, rows, pad), F32),
                   jax.ShapeDtypeStruct((db, rows, HEAD_DIM), F32),
                   jax.ShapeDtypeStruct((db, rows, HEAD_DIM), jnp.int32)],
        compiler_params=_params("arbitrary"),
        name="select",
    )(qrows, ksum, st, knew, srow)


def _attend_s_kernel(picks_ref, pt_ref, pc_ref, pown_ref, l_ref, vnew_ref, cache_ref, o_ref,
                     v_buf, sem, *, n_q, n_slots, n_pages, page_size):
    h = pl.program_id(1)
    step = pl.program_id(0) * N_HEADS + h
    n_steps = pl.num_programs(0) * N_HEADS
    cur = step % 2
    pages_per_block = MOBA_BLOCK // page_size

    def page_copy(at_step, half, s):
        block = picks_ref[at_step * (n_slots // pages_per_block) + s // pages_per_block]
        page = pt_ref[(at_step // N_HEADS) * n_pages + block * pages_per_block + s % pages_per_block]
        return pltpu.make_async_copy(cache_ref.at[page, :, at_step % N_HEADS, :],
                                     v_buf.at[half, s], sem.at[half])

    @pl.when(step == 0)
    def _():
        for s in range(n_slots):
            page_copy(step, cur, s).start()

    @pl.when(step + 1 < n_steps)
    def _():
        for s in range(n_slots):
            page_copy(step + 1, 1 - cur, s).start()

    heads_per_tile = SUBLANES // ROWS_PER_HEAD
    tile_rows = pl.ds(pl.multiple_of((h // heads_per_tile) * SUBLANES, SUBLANES), SUBLANES)
    base = (h % heads_per_tile) * ROWS_PER_HEAD
    pc = pc_ref[tile_rows, :]
    row = lax.broadcasted_iota(jnp.int32, (SUBLANES, page_size), 0)
    acc = jnp.dot(pown_ref[tile_rows, :].astype(BF16), vnew_ref[...].astype(BF16),
                  preferred_element_type=F32)
    for s in range(n_slots):
        page_copy(step, cur, s).wait()
    slots_per_q = n_slots // n_q
    for q in range(n_q):
        for t in range(slots_per_q):
            p = jnp.where(row == base + q, pc[:, t * page_size:(t + 1) * page_size], 0.0)
            v = v_buf[cur, q * slots_per_q + t].astype(BF16)
            acc = acc + jnp.dot(p.astype(BF16), v, preferred_element_type=F32)
    out = acc / l_ref[tile_rows, 0:1]
    mine = out[:n_q, :]
    for k in range(1, heads_per_tile):
        mine = jnp.where(base == k * ROWS_PER_HEAD,
                         out[k * ROWS_PER_HEAD:k * ROWS_PER_HEAD + n_q, :], mine)
    o_ref[...] = mine


def _attend_s(picks, pt, pc, pown, l, vnew, cache_v, n_q, n_pages):
    db, rows, _ = pc.shape
    page_size = cache_v.shape[1]
    n_slots = n_q * MOBA_TOPK * (MOBA_BLOCK // page_size)
    pad = vnew.shape[1]
    per_b = lambda width: pl.BlockSpec((None, rows, width), lambda b, h, *_: (b, 0, 0))
    grid_spec = pltpu.PrefetchScalarGridSpec(
        num_scalar_prefetch=2,
        grid=(db, N_HEADS),
        in_specs=[per_b(pc.shape[2]), per_b(pad), per_b(HEAD_DIM),
                  pl.BlockSpec((None, pad, HEAD_DIM), lambda b, h, *_: (b, 0, h)),
                  pl.BlockSpec(memory_space=pl.ANY)],
        out_specs=pl.BlockSpec((None, n_q, HEAD_DIM), lambda b, h, *_: (b, 0, h)),
        scratch_shapes=[pltpu.VMEM((2, n_slots, page_size, HEAD_DIM), F32),
                        pltpu.SemaphoreType.DMA((2,))],
    )
    return pl.pallas_call(
        functools.partial(_attend_s_kernel, n_q=n_q, n_slots=n_slots, n_pages=n_pages,
                          page_size=page_size),
        grid_spec=grid_spec,
        out_shape=jax.ShapeDtypeStruct((db, n_q, N_HEADS * HEAD_DIM), F32),
        compiler_params=_params("arbitrary", "arbitrary"),
        name="attend_s",
    )(picks, pt, pc, pown, l, vnew, cache_v)


def _sample_query_rows(q_s, db, n_q):
    assert n_q <= ROWS_PER_HEAD
    rows = ROWS_PER_HEAD * N_HEADS
    q4 = jnp.pad(q_s.reshape(db, n_q, N_HEADS, HEAD_DIM).transpose(0, 2, 1, 3),
                 ((0, 0), (0, 0), (0, ROWS_PER_HEAD - n_q), (0, 0)))
    place = lambda onehot: (q4[:, :, :, None, :] * onehot[None, :, None, :, None]
                            ).reshape(db, rows, onehot.shape[1] * HEAD_DIM)
    heads = jnp.arange(N_HEADS)
    own = (heads[:, None] == heads[None, :]).astype(F32)
    pair = (heads[:, None] % HEAD_PAIR == jnp.arange(HEAD_PAIR)[None, :]).astype(F32)
    return place(own), place(pair)


def _sample_attention(qrows, st, ksum, k_s, v_s, cache_v, pt, slopes, db, n_q, n_pages):
    d_attn = N_HEADS * HEAD_DIM
    rows = ROWS_PER_HEAD * N_HEADS
    pad_new = ((0, 0), (0, HEAD_DIM - n_q), (0, 0))
    knew = jnp.pad(k_s.reshape(db, n_q, d_attn), pad_new)
    vnew = jnp.pad(v_s.reshape(db, n_q, d_attn), pad_new)
    srow = jnp.broadcast_to(jnp.repeat(slopes, ROWS_PER_HEAD)[:, None], (rows, HEAD_DIM))
    pc, pown, l, ids = _select(qrows, ksum, st, knew, srow, n_q)
    picks = ids[:, :, :MOBA_TOPK].reshape(db, N_HEADS, ROWS_PER_HEAD, MOBA_TOPK)[:, :, :n_q]
    return _attend_s(picks.reshape(-1), pt, pc, pown, l, vnew, cache_v, n_q, n_pages
                     ).reshape(db * n_q, d_attn)


HALO_ROWS = 16


def _out_kernel(attn_ref, za_ref, bg_ref, cg_ref, u_ref, zb_ref, ga0_ref, ga1_ref, gb0_ref, gb1_ref,
                h1_ref, h2_ref, x_ref, gate_ref, cw_ref, wpa_ref, wpb_ref, wo_ref, fg_ref,
                y_ref, cu_ref, *, sample, tiles_per_seq, n_q):
    tm = x_ref.shape[0]
    cu = cg_ref[...].astype(F32) * u_ref[...].astype(F32)
    row = lax.broadcasted_iota(jnp.int32, cu.shape, 0)
    prev1 = pltpu.roll(cu, 1, 0)
    prev2 = pltpu.roll(cu, 2, 0)
    if sample:
        t = row % n_q
        prev1 = jnp.where(t >= 1, prev1, h1_ref[...])
        prev2 = jnp.where(t >= 2, prev2, h2_ref[...])
        cu_ref[...] = cu
    else:
        first = (pl.program_id(0) % tiles_per_seq) == 0
        halo = h1_ref[...].astype(F32) * h2_ref[...].astype(F32)
        halo = jnp.where(first, 0.0, halo)
        before1 = halo[HALO_ROWS - 1:HALO_ROWS, :]
        before2 = halo[HALO_ROWS - 2:HALO_ROWS - 1, :]
        prev1 = jnp.where(row >= 1, prev1, before1)
        prev2 = jnp.where(row >= 2, prev2, jnp.where(row == 0, before2, before1))
        cu_ref[...] = cu[tm - (CONV_WIDTH - 1):, :]

    conv = bg_ref[...].astype(F32) * (cw_ref[0:1, :] * prev2 + cw_ref[1:2, :] * prev1
                                      + cw_ref[2:3, :] * cu)
    a_act = (attn_ref[...].astype(F32) * jax.nn.silu(za_ref[...].astype(F32))).astype(BF16)
    b_act = (conv * jax.nn.silu(zb_ref[...].astype(F32))).astype(BF16)
    ya = jnp.dot(a_act, wpa_ref[...], preferred_element_type=F32)
    yb = jnp.dot(b_act, wpb_ref[...], preferred_element_type=F32)
    ga = jnp.concatenate([ga0_ref[...], ga1_ref[...]], axis=1).astype(F32)
    gb = jnp.concatenate([gb0_ref[...], gb1_ref[...]], axis=1).astype(F32)
    merged = jax.nn.sigmoid(ga) * ya + jax.nn.sigmoid(gb) * yb
    out = jnp.dot(merged.astype(BF16), wo_ref[...], preferred_element_type=F32)
    xn = x_ref[...] + gate_ref[...] * out
    y = xn * lax.rsqrt(jnp.mean(xn * xn, axis=-1, keepdims=True) + NORM_EPS)
    y_ref[...] = y * fg_ref[...]


def _out(attn, rest, h1, h2, x, gate, gate_spec, conv_w, w_pa, w_pb, w_o, final_g,
         *, sample, tm, tiles_per_seq=1, n_q=1):
    m, d = x.shape
    dc = attn.shape[1]
    tiles = m // tm
    slab = lambda s: pl.BlockSpec((None, tm, dc), lambda i: (s, i, 0))
    const = lambda shape: pl.BlockSpec(shape, lambda i: (0,) * len(shape),
                                       pipeline_mode=pl.Buffered(1))
    if sample:
        hist_specs = [pl.BlockSpec((tm, dc), lambda i: (i, 0))] * 2
        cu_spec = pl.BlockSpec((tm, dc), lambda i: (i, 0))
        cu_shape = jax.ShapeDtypeStruct((m, dc), F32)
    else:
        per_tile = tm // HALO_ROWS
        halo = lambda s: pl.BlockSpec((None, HALO_ROWS, dc),
                                      lambda i: (s, jnp.maximum(i * per_tile - 1, 0), 0))
        hist_specs = [halo(2), halo(3)]
        cu_spec = pl.BlockSpec((None, CONV_WIDTH - 1, dc), lambda i: (i, 0, 0))
        cu_shape = jax.ShapeDtypeStruct((tiles, CONV_WIDTH - 1, dc), F32)
    return pl.pallas_call(
        functools.partial(_out_kernel, sample=sample, tiles_per_seq=tiles_per_seq, n_q=n_q),
        grid=(tiles,),
        in_specs=[pl.BlockSpec((tm, dc), lambda i: (i, 0))]
        + [slab(s) for s in (0, 1, 2, 3, 4, 5, 6, 7, 8)]
        + hist_specs
        + [pl.BlockSpec((tm, d), lambda i: (i, 0)), gate_spec,
           const(conv_w.shape), const(w_pa.shape), const(w_pb.shape), const(w_o.shape),
           const(final_g.shape)],
        out_specs=[pl.BlockSpec((tm, d), lambda i: (i, 0)), cu_spec],
        out_shape=[jax.ShapeDtypeStruct((m, d), F32), cu_shape],
        compiler_params=_params("arbitrary"),
        name="out",
    )(attn, *([rest] * 9), h1, h2, x, gate, conv_w, w_pa, w_pb, w_o, final_g)


def kernel(x_prompt, x_sample, cache_k, cache_v, state_conv, page_table, c_prompt, c_sample,
           norm_g, w_ada, b_ada, w_in, conv_w, w_pa, w_pb, w_o, final_g):
    depth = norm_g.shape[0]
    assert depth == 1, "single-layer step only"
    batch, seq, d = x_prompt.shape
    db, n_q, _ = x_sample.shape
    n_pool, page_size, n_heads, head_dim = cache_k.shape[1:]
    assert (n_heads, head_dim) == (N_HEADS, HEAD_DIM)
    n_pages = page_table.shape[1]
    past_len = n_pages * page_size
    d_attn = n_heads * head_dim
    d_conv = d - d_attn
    assert d_attn == d_conv == 1024 and w_in.shape[2] == 12 * 1024
    assert seq % MOBA_BLOCK == 0 and past_len % MOBA_BLOCK == 0 and n_q <= MOBA_BLOCK

    slopes = jnp.exp2(-8.0 * jnp.arange(1, N_HEADS + 1, dtype=F32) / N_HEADS)

    n_c = batch + db
    c_all = jnp.pad(jnp.concatenate([c_prompt, c_sample], axis=0), ((0, -n_c % 8), (0, 0)))
    mod = _ada(c_all, w_ada[0], b_ada[0][None, :])
    mod4 = mod.reshape(mod.shape[0], 3, 1, d)
    mod_tok = jnp.repeat(mod[batch:n_c].reshape(db, 3, d), n_q, axis=0)

    xp = x_prompt.reshape(batch * seq, d)
    xs = x_sample.reshape(db * n_q, d)
    g = norm_g[0][None, :]
    tm_prep = 512
    per_seq = seq // tm_prep
    mod_spec = lambda s: pl.BlockSpec((None, None, 1, d), lambda i: (i // per_seq, s, 0, 0))
    h_p = _prep(xp, g, mod4, mod4, mod_spec(1), mod_spec(0), tm_prep)
    tok_spec = pl.BlockSpec((db * n_q, d), lambda i: (0, 0))
    h_s = _prep(xs, g, mod_tok[:, 1], mod_tok[:, 0], tok_spec, tok_spec, db * n_q)

    w = w_in[0]
    (q_p,), (q_s,) = _proj(h_p, h_s, w, 0, 1, F32)
    (k_p,), (k_s,) = _proj(h_p, h_s, w, 1, 1, F32)
    (v_p,), (v_s,) = _proj(h_p, h_s, w, 2, 1, F32)
    rest_p, rest_s = _proj(h_p, h_s, w, 3, 9, BF16)

    ck = cache_k[0].reshape(n_pool, page_size * N_HEADS, HEAD_DIM)
    pt = page_table.reshape(-1).astype(jnp.int32)
    qrows, qpair = _sample_query_rows(q_s, db, n_q)
    attn_p, st, ksum = _attn_p(q_p, k_p, v_p, slopes, batch, seq, pt, qpair, ck, n_pages)
    attn_s = _sample_attention(qrows, st, ksum, k_s, v_s, cache_v[0], pt, slopes, db, n_q, n_pages)

    cw = conv_w[0]
    wpa, wpb, wo = w_pa[0].astype(BF16), w_pb[0].astype(BF16), w_o[0].astype(BF16)
    fg = final_g[None, :]
    tm_out = 256
    tiles_per_seq = seq // tm_out
    gate_p_spec = pl.BlockSpec((None, None, 1, d), lambda i: (i // tiles_per_seq, 2, 0, 0))
    y_p, cu_tail = _out(attn_p, rest_p, rest_p, rest_p, xp, mod4, gate_p_spec, cw, wpa, wpb, wo, fg,
                        sample=False, tm=tm_out, tiles_per_seq=tiles_per_seq)

    state = state_conv[0]
    zeros = jnp.zeros((db, n_q - 1, d_conv), F32)
    hist1 = jnp.concatenate([state[:, 1:2], zeros], axis=1).reshape(db * n_q, d_conv)
    hist2 = jnp.concatenate([state, zeros[:, 1:]], axis=1).reshape(db * n_q, d_conv)
    gate_s_spec = pl.BlockSpec((db * n_q, d), lambda i: (0, 0))
    y_s, cu_s = _out(attn_s, rest_s, hist1, hist2, xs, mod_tok[:, 2], gate_s_spec, cw, wpa, wpb, wo, fg,
                     sample=True, tm=db * n_q, n_q=n_q)

    conv_p = cu_tail.reshape(batch, tiles_per_seq, CONV_WIDTH - 1, d_conv)[:, -1]
    conv_s = cu_s.reshape(db, n_q, d_conv)[:, n_q - (CONV_WIDTH - 1):]
    kv_p = (1, batch, seq, N_HEADS, HEAD_DIM)
    kv_s = (1, db, n_q, N_HEADS, HEAD_DIM)
    return (y_p.reshape(batch, seq, d), y_s.reshape(db, n_q, d),
            k_p.reshape(kv_p), v_p.reshape(kv_p), conv_p[None],
            k_s.reshape(kv_s), v_s.reshape(kv_s), conv_s[None])
```

```python
import functools

import jax
import jax.numpy as jnp
from jax import lax
from jax.experimental import pallas as pl
from jax.experimental.pallas import tpu as pltpu

F32 = jnp.float32
BF16 = jnp.bfloat16

N_HEADS = 8
HEAD_DIM = 128
MOBA_BLOCK = 256
MOBA_TOPK = 3
CONV_WIDTH = 3
NORM_EPS = 1e-6
NEG = float(jnp.finfo(jnp.float32).min)
MASKED = -1e30
ATTN_SCALE = HEAD_DIM ** -0.5
LOG2_E = 1.4426950408889634
SCALE2 = ATTN_SCALE * LOG2_E
Q_TILES = 2

VMEM_LIMIT_BYTES = 56 * 1024 * 1024

_NT = (((1,), (1,)), ((), ()))


def _params(*semantics):
    return pltpu.CompilerParams(dimension_semantics=semantics, vmem_limit_bytes=VMEM_LIMIT_BYTES)


def _ada_kernel(c_ref, w_ref, b_ref, o_ref):
    o_ref[...] = jnp.dot(c_ref[...].astype(BF16), w_ref[...].astype(BF16),
                         preferred_element_type=F32) + b_ref[...]


def _ada(c_all, w_ada, b_ada):
    rows, d = c_all.shape
    n = w_ada.shape[1]
    tn = 1024
    return pl.pallas_call(
        _ada_kernel,
        grid=(n // tn,),
        in_specs=[pl.BlockSpec((rows, d), lambda j: (0, 0)),
                  pl.BlockSpec((d, tn), lambda j: (0, j)),
                  pl.BlockSpec((1, tn), lambda j: (0, j))],
        out_specs=pl.BlockSpec((rows, tn), lambda j: (0, j)),
        out_shape=jax.ShapeDtypeStruct((rows, n), F32),
        compiler_params=_params("arbitrary"),
        name="ada",
    )(c_all, w_ada, b_ada)


def _prep_kernel(x_ref, g_ref, scale_ref, shift_ref, h_ref):
    x = x_ref[...]
    y = x * lax.rsqrt(jnp.mean(x * x, axis=-1, keepdims=True) + NORM_EPS)
    h = (y * g_ref[...]) * (1.0 + scale_ref[...]) + shift_ref[...]
    h_ref[...] = h.astype(BF16)


def _prep(x, g, scale, shift, scale_spec, shift_spec, tm):
    m, d = x.shape
    return pl.pallas_call(
        _prep_kernel,
        grid=(m // tm,),
        in_specs=[pl.BlockSpec((tm, d), lambda i: (i, 0)),
                  pl.BlockSpec((1, d), lambda i: (0, 0)),
                  scale_spec, shift_spec],
        out_specs=pl.BlockSpec((tm, d), lambda i: (i, 0)),
        out_shape=jax.ShapeDtypeStruct((m, d), BF16),
        compiler_params=_params("arbitrary"),
        name="prep",
    )(x, g, scale, shift)


def _proj_kernel(hp_ref, hs_ref, w_ref, op_ref, os_ref, wb_ref):
    @pl.when(pl.program_id(1) == 0)
    def _():
        wb_ref[...] = w_ref[...].astype(BF16)
        os_ref[...] = jnp.dot(hs_ref[...], wb_ref[...], preferred_element_type=F32).astype(os_ref.dtype)

    op_ref[...] = jnp.dot(hp_ref[...], wb_ref[...], preferred_element_type=F32).astype(op_ref.dtype)


def _proj(h_p, h_s, w, col0, ncol, dtype, tm=1024, tn=1024):
    mp, d = h_p.shape
    ms = h_s.shape[0]
    return pl.pallas_call(
        _proj_kernel,
        grid=(ncol, mp // tm),
        in_specs=[pl.BlockSpec((tm, d), lambda j, i: (i, 0)),
                  pl.BlockSpec((ms, d), lambda j, i: (0, 0)),
                  pl.BlockSpec((d, tn), lambda j, i: (0, col0 + j))],
        out_specs=[pl.BlockSpec((None, tm, tn), lambda j, i: (j, i, 0)),
                   pl.BlockSpec((None, ms, tn), lambda j, i: (j, 0, 0))],
        out_shape=[jax.ShapeDtypeStruct((ncol, mp, tn), dtype),
                   jax.ShapeDtypeStruct((ncol, ms, tn), dtype)],
        scratch_shapes=[pltpu.VMEM((d, tn), BF16)],
        compiler_params=_params("arbitrary", "arbitrary"),
        name="proj",
    )(h_p, h_s, w)


def _topk_mask(gate, valid, idx_f, n, axis):
    g = jnp.where(valid, gate, -jnp.inf)
    sel = jnp.zeros(gate.shape, F32)
    for _ in range(MOBA_TOPK):
        m = jnp.max(g, axis=axis, keepdims=True)
        first = jnp.min(jnp.where(g == m, idx_f, float(n)), axis=axis, keepdims=True)
        hit = idx_f == first
        sel = jnp.where(hit & valid, 1.0, sel)
        g = jnp.where(hit, -jnp.inf, g)
    return sel


SUBLANES = 8
ROWS_PER_HEAD = 4
HEAD_PAIR = N_HEADS // 2
PAIRED = N_HEADS // HEAD_PAIR


def _kstream_repack(k_refs, ksum_ref, kb_ref, page_size):
    pages_per_block = MOBA_BLOCK // page_size
    pair_rows = PAIRED * page_size
    row_hh = lax.broadcasted_iota(jnp.int32, (SUBLANES, HEAD_DIM), 0) % PAIRED
    for blk in range(len(k_refs) // pages_per_block):
        for g in range(HEAD_PAIR):
            tot = None
            for j in range(blk * pages_per_block, (blk + 1) * pages_per_block):
                x = k_refs[j][pl.ds(g, pair_rows, stride=HEAD_PAIR), :]
                kb_ref[j * pair_rows:(j + 1) * pair_rows, g * HEAD_DIM:(g + 1) * HEAD_DIM] = (
                    x.astype(BF16))
                part = jnp.sum(x.reshape(pair_rows // SUBLANES, SUBLANES, HEAD_DIM), axis=0)
                tot = part if tot is None else tot + part
            for hh in range(PAIRED):
                head = g + HEAD_PAIR * hh
                ksum_ref[blk:blk + 1, head * HEAD_DIM:(head + 1) * HEAD_DIM] = jnp.sum(
                    jnp.where(row_hh == hh, tot, 0.0), axis=0, keepdims=True)


def _kstream_scores(qpair_ref, kb_ref, st_ref):
    st_ref[...] = lax.dot_general(qpair_ref[...].astype(BF16), kb_ref[...], _NT,
                                  preferred_element_type=F32)


def _attn_p_kernel(pt_ref, slopes_ref, q_ref, k_ref, v_ref, qrows_ref, *refs,
                   n_blocks, n_stream, page_size):
    kp_refs = refs[:n_stream]
    o_ref, st_ref, ksum_ref, ka_ref, vb_ref, kmean_ref, eye_ref, qa_ref, s_ref, kb_ref = refs[n_stream:]
    blk = MOBA_BLOCK
    half = blk // 2
    h = pl.program_id(1)
    group = pl.program_id(2)

    @pl.when(group == 0)
    def _():
        lane = lax.broadcasted_iota(jnp.int32, (blk, HEAD_DIM), 1)
        for n in range(n_blocks):
            rows = slice(n * blk, (n + 1) * blk)
            kf = k_ref[rows, :]
            ka_ref[rows, :HEAD_DIM] = kf.astype(BF16)
            ka_ref[rows, HEAD_DIM:] = jnp.where(lane == n, 1.0, 0.0).astype(BF16)
            vb_ref[rows, :] = v_ref[rows, :].astype(BF16)
            kmean_ref[n:n + 1, :] = jnp.sum(kf, axis=0, keepdims=True) * (1.0 / blk)
        eye_ref[...] = jnp.where(lax.broadcasted_iota(jnp.int32, (blk, blk), 0)
                                 == lax.broadcasted_iota(jnp.int32, (blk, blk), 1),
                                 1.0, 0.0).astype(BF16)

    _kstream_repack(kp_refs, ksum_ref, kb_ref, page_size)

    slope2 = slopes_ref[h] * LOG2_E
    for t in range(Q_TILES):
        qi = group * Q_TILES + t
        qf = q_ref[t * blk:(t + 1) * blk, :]
        gate_t = lax.dot_general(kmean_ref[...], qf, _NT, precision=lax.Precision.HIGHEST,
                                 preferred_element_type=F32)
        row = lax.broadcasted_iota(jnp.int32, gate_t.shape, 0)
        sel_t = _topk_mask(gate_t, row < qi, row.astype(F32), n_blocks, axis=0)
        unpicked_t = jnp.concatenate([jnp.where(sel_t > 0.5, 0.0, MASKED),
                                      jnp.zeros((HEAD_DIM - n_blocks, blk), F32)], axis=0)
        unpicked = lax.dot_general(eye_ref[...], unpicked_t.astype(BF16), _NT,
                                   preferred_element_type=F32)
        qa_ref[t, :, :HEAD_DIM] = qf.astype(BF16)
        qa_ref[t, :, HEAD_DIM:] = unpicked.astype(BF16)
    _kstream_scores(qrows_ref, kb_ref, st_ref)

    def halves(x):
        return x[:, :half], x[:, half:]

    def attend(first_qi):
        key_col = lax.broadcasted_iota(jnp.int32, (1, blk), 1).astype(F32)
        col_bias = slope2 * key_col
        rel = (lax.broadcasted_iota(jnp.int32, (blk, blk), 0)
               - lax.broadcasted_iota(jnp.int32, (blk, blk), 1))
        tiles = [(t, first_qi + t) for t in range(Q_TILES)]

        own, m = {}, {}
        for t, qi in tiles:
            s_own = lax.dot_general(qa_ref[t, :, :HEAD_DIM], ka_ref[qi * blk:(qi + 1) * blk, :HEAD_DIM],
                                    _NT, preferred_element_type=F32) * SCALE2 + col_bias
            own[t] = jnp.where(rel >= 0, s_own, NEG)
            lo, hi = halves(own[t])
            mx = jnp.maximum(lo, hi)
            for n in range(qi):
                bias_n = col_bias + slope2 * float((n - qi) * blk)
                s = lax.dot_general(qa_ref[t], ka_ref[n * blk:(n + 1) * blk, :], _NT,
                                    preferred_element_type=F32) * SCALE2 + bias_n
                s_ref[t, n] = s
                lo, hi = halves(s)
                mx = jnp.maximum(mx, jnp.maximum(lo, hi))
            m[t] = jnp.broadcast_to(jnp.max(mx, axis=1, keepdims=True), (blk, half))

        def probs(s, m_t):
            lo, hi = halves(s)
            lo, hi = jnp.exp2(lo - m_t), jnp.exp2(hi - m_t)
            return lo + hi, jnp.concatenate([lo, hi], axis=1).astype(BF16)

        for t, qi in tiles:
            lsum, p = probs(own[t], m[t])
            acc = jnp.dot(p, vb_ref[qi * blk:(qi + 1) * blk, :], preferred_element_type=F32)
            for n in range(qi):
                part, p = probs(s_ref[t, n], m[t])
                lsum = lsum + part
                acc = acc + jnp.dot(p, vb_ref[n * blk:(n + 1) * blk, :], preferred_element_type=F32)
            l = jnp.sum(lsum, axis=1, keepdims=True)
            o_ref[t * blk:(t + 1) * blk, :] = (acc / l).astype(o_ref.dtype)

    for g in range(n_blocks // Q_TILES):
        pl.when(group == g)(functools.partial(attend, g * Q_TILES))


def _attn_p(q, k, v, slopes, batch, seq, pt, qpair, cache_k, n_pages):
    n_blocks = seq // MOBA_BLOCK
    blk = MOBA_BLOCK
    assert n_blocks % Q_TILES == 0 and n_blocks <= HEAD_DIM
    n_groups = n_blocks // Q_TILES
    db, rows, pair_width = qpair.shape
    width = N_HEADS * HEAD_DIM
    page_size = cache_k.shape[1] // N_HEADS
    n_steps = batch * N_HEADS * n_groups
    n_stream = db * n_pages // n_steps
    assert n_stream * n_steps == db * n_pages and n_pages % n_stream == 0
    assert (n_stream * page_size) % MOBA_BLOCK == 0
    steps_per_sb = n_pages // n_stream
    keys_per_step = n_stream * page_size
    blocks_per_step = keys_per_step // MOBA_BLOCK

    def stream_pos(b, h, qi):
        lin = (b * N_HEADS + h) * n_groups + qi
        return lin // steps_per_sb, lin % steps_per_sb

    def page_spec(j):
        def index(b, h, qi, pt):
            sb, c = stream_pos(b, h, qi)
            return pt[sb * n_pages + c * n_stream + j], 0, 0
        return pl.BlockSpec((None, page_size * N_HEADS, HEAD_DIM), index)

    kv_spec = pl.BlockSpec((seq, HEAD_DIM), lambda b, h, qi, pt: (b, h))
    q_spec = pl.BlockSpec((Q_TILES * blk, HEAD_DIM), lambda b, h, qi, pt: (b * n_groups + qi, h))
    grid_spec = pltpu.PrefetchScalarGridSpec(
        num_scalar_prefetch=1,
        grid=(batch, N_HEADS, n_groups),
        in_specs=[pl.BlockSpec(memory_space=pltpu.SMEM), q_spec, kv_spec, kv_spec,
                  pl.BlockSpec((None, rows, pair_width),
                               lambda b, h, qi, pt: (stream_pos(b, h, qi)[0], 0, 0))]
        + [page_spec(j) for j in range(n_stream)],
        out_specs=[q_spec,
                   pl.BlockSpec((None, rows, PAIRED * keys_per_step),
                                lambda b, h, qi, pt: (stream_pos(b, h, qi)[0], 0, stream_pos(b, h, qi)[1])),
                   pl.BlockSpec((None, None, blocks_per_step, width),
                                lambda b, h, qi, pt: stream_pos(b, h, qi) + (0, 0))],
        scratch_shapes=[pltpu.VMEM((seq, 2 * HEAD_DIM), BF16),
                        pltpu.VMEM((seq, HEAD_DIM), BF16),
                        pltpu.VMEM((n_blocks, HEAD_DIM), F32),
                        pltpu.VMEM((blk, blk), BF16),
                        pltpu.VMEM((Q_TILES, blk, 2 * HEAD_DIM), BF16),
                        pltpu.VMEM((Q_TILES, n_blocks, blk, blk), F32),
                        pltpu.VMEM((PAIRED * keys_per_step, pair_width), BF16)],
    )
    attn, st, ksum = pl.pallas_call(
        functools.partial(_attn_p_kernel, n_blocks=n_blocks, n_stream=n_stream, page_size=page_size),
        grid_spec=grid_spec,
        out_shape=[jax.ShapeDtypeStruct(q.shape, BF16),
                   jax.ShapeDtypeStruct((db, rows, PAIRED * n_pages * page_size), F32),
                   jax.ShapeDtypeStruct((db, steps_per_sb, blocks_per_step, width), F32)],
        compiler_params=_params("arbitrary", "arbitrary", "arbitrary"),
        name="attn_p",
    )(pt, slopes, q, k, v, qpair, *([cache_k] * n_stream))
    return attn, st, ksum.reshape(db, steps_per_sb * blocks_per_step, width)


def _dot_onehot_exact(x, onehot):
    hi = x.astype(BF16)
    rest = x - hi.astype(F32)
    mid = rest.astype(BF16)
    lo = (rest - mid.astype(F32)).astype(BF16)
    return ((jnp.dot(hi, onehot, preferred_element_type=F32)
             + jnp.dot(mid, onehot, preferred_element_type=F32))
            + jnp.dot(lo, onehot, preferred_element_type=F32))


def _select_kernel(qrows_ref, ksum_ref, st_ref, knew_ref, srow_ref,
                   pc_ref, pown_ref, l_ref, ids_ref, *, n_new):
    rows = st_ref.shape[0]
    n_blocks = ksum_ref.shape[0]
    blk = MOBA_BLOCK
    past_len = n_blocks * blk
    pblk = PAIRED * blk
    qr = qrows_ref[...]
    kmean = ksum_ref[...] * (1.0 / blk)
    gate = lax.dot_general(qr, kmean, _NT, precision=lax.Precision.HIGHEST,
                           preferred_element_type=F32)
    lane_f = lax.broadcasted_iota(jnp.int32, gate.shape, 1).astype(F32)
    expand = (lax.broadcasted_iota(jnp.int32, (n_blocks, n_blocks * pblk), 1) // pblk
              == lax.broadcasted_iota(jnp.int32, (n_blocks, n_blocks * pblk), 0)).astype(BF16)
    own_col = (lax.broadcasted_iota(jnp.int32, (rows, pblk), 1) % PAIRED
               == lax.broadcasted_iota(jnp.int32, (rows, pblk), 0) // (ROWS_PER_HEAD * HEAD_PAIR))
    compact = (lax.broadcasted_iota(jnp.int32, (pblk, blk), 0) // PAIRED
               == lax.broadcasted_iota(jnp.int32, (pblk, blk), 1)).astype(BF16)

    slope = srow_ref[:, 0:1]
    qq = lax.broadcasted_iota(jnp.int32, (rows, 1), 0) % ROWS_PER_HEAD
    q_pos = (past_len + qq).astype(F32)
    col = lax.broadcasted_iota(jnp.int32, (1, blk), 1).astype(F32)
    st = st_ref[...]

    g = gate
    firsts, scores = [], []
    for _ in range(MOBA_TOPK):
        m = jnp.max(g, axis=1, keepdims=True)
        first = jnp.min(jnp.where(g == m, lane_f, float(n_blocks)), axis=1, keepdims=True)
        hit = lane_f == first
        g = jnp.where(hit, -jnp.inf, g)
        keep = jnp.dot(jnp.where(hit, 1.0, 0.0).astype(BF16), expand, preferred_element_type=F32)
        kept = st * keep
        sc = kept[:, :pblk]
        for n in range(1, n_blocks):
            sc = sc + kept[:, n * pblk:(n + 1) * pblk]
        sc = _dot_onehot_exact(jnp.where(own_col, sc, 0.0), compact)
        dist = (q_pos - first * float(blk)) - col
        scores.append(sc * ATTN_SCALE - slope * dist)
        firsts.append(first)

    j = lax.broadcasted_iota(jnp.int32, (rows, knew_ref.shape[0]), 1)
    so = lax.dot_general(qr.astype(BF16), knew_ref[...].astype(BF16), _NT,
                         preferred_element_type=F32) * ATTN_SCALE
    so = so - slope * (qq - j).astype(F32)
    so = jnp.where((j <= qq) & (j < n_new), so, NEG)

    m = jnp.max(so, axis=1, keepdims=True)
    for s in scores:
        m = jnp.maximum(m, jnp.max(s, axis=1, keepdims=True))
    po = jnp.exp(so - m)
    l = jnp.sum(po, axis=1, keepdims=True)
    for t, s in enumerate(scores):
        p = jnp.exp(s - m)
        l = l + jnp.sum(p, axis=1, keepdims=True)
        pc_ref[:, t * blk:(t + 1) * blk] = p
    pown_ref[...] = po
    l_ref[...] = jnp.broadcast_to(l, l_ref.shape)
    out_lane = lax.broadcasted_iota(jnp.int32, ids_ref.shape, 1)
    ids = jnp.zeros(ids_ref.shape, F32)
    for t, first in enumerate(firsts):
        ids = jnp.where(out_lane == t, first, ids)
    ids_ref[...] = ids.astype(jnp.int32)


def _select(qrows, ksum, st, knew, srow, n_new):
    db, rows, width = qrows.shape
    n_blocks = ksum.shape[1]
    past_len = st.shape[2]
    pad = knew.shape[1]
    assert n_blocks >= MOBA_TOPK and past_len == PAIRED * n_blocks * MOBA_BLOCK
    per_b = lambda *shape: pl.BlockSpec((None,) + shape, lambda b: (b,) + (0,) * len(shape))
    return pl.pallas_call(
        functools.partial(_select_kernel, n_new=n_new),
        grid=(db,),
        in_specs=[per_b(rows, width), per_b(n_blocks, width), per_b(rows, past_len),
                  per_b(pad, width), pl.BlockSpec((rows, HEAD_DIM), lambda b: (0, 0))],
        out_specs=[per_b(rows, MOBA_TOPK * MOBA_BLOCK), per_b(rows, pad), per_b(rows, HEAD_DIM),
                   per_b(rows, HEAD_DIM)],
        out_shape=[jax.ShapeDtypeStruct((db, rows, MOBA_TOPK * MOBA_BLOCK), F32),
                   jax.ShapeDtypeStruct((db, rows, pad), F32),
                   jax.ShapeDtypeStruct((db, rows, HEAD_DIM), F32),
                   jax.ShapeDtypeStruct((db, rows, HEAD_DIM), jnp.int32)],
        compiler_params=_params("arbitrary"),
        name="select",
    )(qrows, ksum, st, knew, srow)


def _attend_s_kernel(picks_ref, pt_ref, pc_ref, pown_ref, l_ref, vnew_ref, cache_ref, o_ref,
                     v_buf, sem, *, n_q, n_slots, n_pages, page_size):
    h = pl.program_id(1)
    step = pl.program_id(0) * N_HEADS + h
    n_steps = pl.num_programs(0) * N_HEADS
    cur = step % 2
    pages_per_block = MOBA_BLOCK // page_size

    def page_copy(at_step, half, s):
        block = picks_ref[at_step * (n_slots // pages_per_block) + s // pages_per_block]
        page = pt_ref[(at_step // N_HEADS) * n_pages + block * pages_per_block + s % pages_per_block]
        return pltpu.make_async_copy(cache_ref.at[page, :, at_step % N_HEADS, :],
                                     v_buf.at[half, s], sem.at[half])

    @pl.when(step == 0)
    def _():
        for s in range(n_slots):
            page_copy(step, cur, s).start()

    @pl.when(step + 1 < n_steps)
    def _():
        for s in range(n_slots):
            page_copy(step + 1, 1 - cur, s).start()

    heads_per_tile = SUBLANES // ROWS_PER_HEAD
    tile_rows = pl.ds(pl.multiple_of((h // heads_per_tile) * SUBLANES, SUBLANES), SUBLANES)
    base = (h % heads_per_tile) * ROWS_PER_HEAD
    pc = pc_ref[tile_rows, :]
    row = lax.broadcasted_iota(jnp.int32, (SUBLANES, page_size), 0)
    acc = jnp.dot(pown_ref[tile_rows, :].astype(BF16), vnew_ref[...].astype(BF16),
                  preferred_element_type=F32)
    for s in range(n_slots):
        page_copy(step, cur, s).wait()
    slots_per_q = n_slots // n_q
    for q in range(n_q):
        for t in range(slots_per_q):
            p = jnp.where(row == base + q, pc[:, t * page_size:(t + 1) * page_size], 0.0)
            v = v_buf[cur, q * slots_per_q + t].astype(BF16)
            acc = acc + jnp.dot(p.astype(BF16), v, preferred_element_type=F32)
    out = acc / l_ref[tile_rows, 0:1]
    mine = out[:n_q, :]
    for k in range(1, heads_per_tile):
        mine = jnp.where(base == k * ROWS_PER_HEAD,
                         out[k * ROWS_PER_HEAD:k * ROWS_PER_HEAD + n_q, :], mine)
    o_ref[...] = mine


def _attend_s(picks, pt, pc, pown, l, vnew, cache_v, n_q, n_pages):
    db, rows, _ = pc.shape
    page_size = cache_v.shape[1]
    n_slots = n_q * MOBA_TOPK * (MOBA_BLOCK // page_size)
    pad = vnew.shape[1]
    per_b = lambda width: pl.BlockSpec((None, rows, width), lambda b, h, *_: (b, 0, 0))
    grid_spec = pltpu.PrefetchScalarGridSpec(
        num_scalar_prefetch=2,
        grid=(db, N_HEADS),
        in_specs=[per_b(pc.shape[2]), per_b(pad), per_b(HEAD_DIM),
                  pl.BlockSpec((None, pad, HEAD_DIM), lambda b, h, *_: (b, 0, h)),
                  pl.BlockSpec(memory_space=pl.ANY)],
        out_specs=pl.BlockSpec((None, n_q, HEAD_DIM), lambda b, h, *_: (b, 0, h)),
        scratch_shapes=[pltpu.VMEM((2, n_slots, page_size, HEAD_DIM), F32),
                        pltpu.SemaphoreType.DMA((2,))],
    )
    return pl.pallas_call(
        functools.partial(_attend_s_kernel, n_q=n_q, n_slots=n_slots, n_pages=n_pages,
                          page_size=page_size),
        grid_spec=grid_spec,
        out_shape=jax.ShapeDtypeStruct((db, n_q, N_HEADS * HEAD_DIM), F32),
        compiler_params=_params("arbitrary", "arbitrary"),
        name="attend_s",
    )(picks, pt, pc, pown, l, vnew, cache_v)


def _sample_query_rows(q_s, db, n_q):
    assert n_q <= ROWS_PER_HEAD
    rows = ROWS_PER_HEAD * N_HEADS
    q4 = jnp.pad(q_s.reshape(db, n_q, N_HEADS, HEAD_DIM).transpose(0, 2, 1, 3),
                 ((0, 0), (0, 0), (0, ROWS_PER_HEAD - n_q), (0, 0)))
    place = lambda onehot: (q4[:, :, :, None, :] * onehot[None, :, None, :, None]
                            ).reshape(db, rows, onehot.shape[1] * HEAD_DIM)
    heads = jnp.arange(N_HEADS)
    own = (heads[:, None] == heads[None, :]).astype(F32)
    pair = (heads[:, None] % HEAD_PAIR == jnp.arange(HEAD_PAIR)[None, :]).astype(F32)
    return place(own), place(pair)


def _sample_attention(qrows, st, ksum, k_s, v_s, cache_v, pt, slopes, db, n_q, n_pages):
    d_attn = N_HEADS * HEAD_DIM
    rows = ROWS_PER_HEAD * N_HEADS
    pad_new = ((0, 0), (0, HEAD_DIM - n_q), (0, 0))
    knew = jnp.pad(k_s.reshape(db, n_q, d_attn), pad_new)
    vnew = jnp.pad(v_s.reshape(db, n_q, d_attn), pad_new)
    srow = jnp.broadcast_to(jnp.repeat(slopes, ROWS_PER_HEAD)[:, None], (rows, HEAD_DIM))
    pc, pown, l, ids = _select(qrows, ksum, st, knew, srow, n_q)
    picks = ids[:, :, :MOBA_TOPK].reshape(db, N_HEADS, ROWS_PER_HEAD, MOBA_TOPK)[:, :, :n_q]
    return _attend_s(picks.reshape(-1), pt, pc, pown, l, vnew, cache_v, n_q, n_pages
                     ).reshape(db * n_q, d_attn)


HALO_ROWS = 16


def _out_kernel(attn_ref, za_ref, bg_ref, cg_ref, u_ref, zb_ref, ga0_ref, ga1_ref, gb0_ref, gb1_ref,
                h1_ref, h2_ref, x_ref, gate_ref, cw_ref, wpa_ref, wpb_ref, wo_ref, fg_ref,
                y_ref, cu_ref, *, sample, tiles_per_seq, n_q):
    tm = x_ref.shape[0]
    cu = cg_ref[...].astype(F32) * u_ref[...].astype(F32)
    row = lax.broadcasted_iota(jnp.int32, cu.shape, 0)
    prev1 = pltpu.roll(cu, 1, 0)
    prev2 = pltpu.roll(cu, 2, 0)
    if sample:
        t = row % n_q
        prev1 = jnp.where(t >= 1, prev1, h1_ref[...])
        prev2 = jnp.where(t >= 2, prev2, h2_ref[...])
        cu_ref[...] = cu
    else:
        first = (pl.program_id(0) % tiles_per_seq) == 0
        halo = h1_ref[...].astype(F32) * h2_ref[...].astype(F32)
        halo = jnp.where(first, 0.0, halo)
        before1 = halo[HALO_ROWS - 1:HALO_ROWS, :]
        before2 = halo[HALO_ROWS - 2:HALO_ROWS - 1, :]
        prev1 = jnp.where(row >= 1, prev1, before1)
        prev2 = jnp.where(row >= 2, prev2, jnp.where(row == 0, before2, before1))
        cu_ref[...] = cu[tm - (CONV_WIDTH - 1):, :]

    conv = bg_ref[...].astype(F32) * (cw_ref[0:1, :] * prev2 + cw_ref[1:2, :] * prev1
                                      + cw_ref[2:3, :] * cu)
    a_act = (attn_ref[...].astype(F32) * jax.nn.silu(za_ref[...].astype(F32))).astype(BF16)
    b_act = (conv * jax.nn.silu(zb_ref[...].astype(F32))).astype(BF16)
    ya = jnp.dot(a_act, wpa_ref[...], preferred_element_type=F32)
    yb = jnp.dot(b_act, wpb_ref[...], preferred_element_type=F32)
    ga = jnp.concatenate([ga0_ref[...], ga1_ref[...]], axis=1).astype(F32)
    gb = jnp.concatenate([gb0_ref[...], gb1_ref[...]], axis=1).astype(F32)
    merged = jax.nn.sigmoid(ga) * ya + jax.nn.sigmoid(gb) * yb
    out = jnp.dot(merged.astype(BF16), wo_ref[...], preferred_element_type=F32)
    xn = x_ref[...] + gate_ref[...] * out
    y = xn * lax.rsqrt(jnp.mean(xn * xn, axis=-1, keepdims=True) + NORM_EPS)
    y_ref[...] = y * fg_ref[...]


def _out(attn, rest, h1, h2, x, gate, gate_spec, conv_w, w_pa, w_pb, w_o, final_g,
         *, sample, tm, tiles_per_seq=1, n_q=1):
    m, d = x.shape
    dc = attn.shape[1]
    tiles = m // tm
    slab = lambda s: pl.BlockSpec((None, tm, dc), lambda i: (s, i, 0))
    const = lambda shape: pl.BlockSpec(shape, lambda i: (0,) * len(shape),
                                       pipeline_mode=pl.Buffered(1))
    if sample:
        hist_specs = [pl.BlockSpec((tm, dc), lambda i: (i, 0))] * 2
        cu_spec = pl.BlockSpec((tm, dc), lambda i: (i, 0))
        cu_shape = jax.ShapeDtypeStruct((m, dc), F32)
    else:
        per_tile = tm // HALO_ROWS
        halo = lambda s: pl.BlockSpec((None, HALO_ROWS, dc),
                                      lambda i: (s, jnp.maximum(i * per_tile - 1, 0), 0))
        hist_specs = [halo(2), halo(3)]
        cu_spec = pl.BlockSpec((None, CONV_WIDTH - 1, dc), lambda i: (i, 0, 0))
        cu_shape = jax.ShapeDtypeStruct((tiles, CONV_WIDTH - 1, dc), F32)
    return pl.pallas_call(
        functools.partial(_out_kernel, sample=sample, tiles_per_seq=tiles_per_seq, n_q=n_q),
        grid=(tiles,),
        in_specs=[pl.BlockSpec((tm, dc), lambda i: (i, 0))]
        + [slab(s) for s in (0, 1, 2, 3, 4, 5, 6, 7, 8)]
        + hist_specs
        + [pl.BlockSpec((tm, d), lambda i: (i, 0)), gate_spec,
           const(conv_w.shape), const(w_pa.shape), const(w_pb.shape), const(w_o.shape),
           const(final_g.shape)],
        out_specs=[pl.BlockSpec((tm, d), lambda i: (i, 0)), cu_spec],
        out_shape=[jax.ShapeDtypeStruct((m, d), F32), cu_shape],
        compiler_params=_params("arbitrary"),
        name="out",
    )(attn, *([rest] * 9), h1, h2, x, gate, conv_w, w_pa, w_pb, w_o, final_g)


def kernel(x_prompt, x_sample, cache_k, cache_v, state_conv, page_table, c_prompt, c_sample,
           norm_g, w_ada, b_ada, w_in, conv_w, w_pa, w_pb, w_o, final_g):
    depth = norm_g.shape[0]
    assert depth == 1, "single-layer step only"
    batch, seq, d = x_prompt.shape
    db, n_q, _ = x_sample.shape
    n_pool, page_size, n_heads, head_dim = cache_k.shape[1:]
    assert (n_heads, head_dim) == (N_HEADS, HEAD_DIM)
    n_pages = page_table.shape[1]
    past_len = n_pages * page_size
    d_attn = n_heads * head_dim
    d_conv = d - d_attn
    assert d_attn == d_conv == 1024 and w_in.shape[2] == 12 * 1024
    assert seq % MOBA_BLOCK == 0 and past_len % MOBA_BLOCK == 0 and n_q <= MOBA_BLOCK

    slopes = jnp.exp2(-8.0 * jnp.arange(1, N_HEADS + 1, dtype=F32) / N_HEADS)

    n_c = batch + db
    c_all = jnp.pad(jnp.concatenate([c_prompt, c_sample], axis=0), ((0, -n_c % 8), (0, 0)))
    mod = _ada(c_all, w_ada[0], b_ada[0][None, :])
    mod4 = mod.reshape(mod.shape[0], 3, 1, d)
    mod_tok = jnp.repeat(mod[batch:n_c].reshape(db, 3, d), n_q, axis=0)

    xp = x_prompt.reshape(batch * seq, d)
    xs = x_sample.reshape(db * n_q, d)
    g = norm_g[0][None, :]
    tm_prep = 512
    per_seq = seq // tm_prep
    mod_spec = lambda s: pl.BlockSpec((None, None, 1, d), lambda i: (i // per_seq, s, 0, 0))
    h_p = _prep(xp, g, mod4, mod4, mod_spec(1), mod_spec(0), tm_prep)
    tok_spec = pl.BlockSpec((db * n_q, d), lambda i: (0, 0))
    h_s = _prep(xs, g, mod_tok[:, 1], mod_tok[:, 0], tok_spec, tok_spec, db * n_q)

    w = w_in[0]
    (q_p,), (q_s,) = _proj(h_p, h_s, w, 0, 1, F32)
    (k_p,), (k_s,) = _proj(h_p, h_s, w, 1, 1, F32)
    (v_p,), (v_s,) = _proj(h_p, h_s, w, 2, 1, F32)
    rest_p, rest_s = _proj(h_p, h_s, w, 3, 9, BF16)

    ck = cache_k[0].reshape(n_pool, page_size * N_HEADS, HEAD_DIM)
    pt = page_table.reshape(-1).astype(jnp.int32)
    qrows, qpair = _sample_query_rows(q_s, db, n_q)
    attn_p, st, ksum = _attn_p(q_p, k_p, v_p, slopes, batch, seq, pt, qpair, ck, n_pages)
    attn_s = _sample_attention(qrows, st, ksum, k_s, v_s, cache_v[0], pt, slopes, db, n_q, n_pages)

    cw = conv_w[0]
    wpa, wpb, wo = w_pa[0].astype(BF16), w_pb[0].astype(BF16), w_o[0].astype(BF16)
    fg = final_g[None, :]
    tm_out = 256
    tiles_per_seq = seq // tm_out
    gate_p_spec = pl.BlockSpec((None, None, 1, d), lambda i: (i // tiles_per_seq, 2, 0, 0))
    y_p, cu_tail = _out(attn_p, rest_p, rest_p, rest_p, xp, mod4, gate_p_spec, cw, wpa, wpb, wo, fg,
                        sample=False, tm=tm_out, tiles_per_seq=tiles_per_seq)

    state = state_conv[0]
    zeros = jnp.zeros((db, n_q - 1, d_conv), F32)
    hist1 = jnp.concatenate([state[:, 1:2], zeros], axis=1).reshape(db * n_q, d_conv)
    hist2 = jnp.concatenate([state, zeros[:, 1:]], axis=1).reshape(db * n_q, d_conv)
    gate_s_spec = pl.BlockSpec((db * n_q, d), lambda i: (0, 0))
    y_s, cu_s = _out(attn_s, rest_s, hist1, hist2, xs, mod_tok[:, 2], gate_s_spec, cw, wpa, wpb, wo, fg,
                     sample=True, tm=db * n_q, n_q=n_q)

    conv_p = cu_tail.reshape(batch, tiles_per_seq, CONV_WIDTH - 1, d_conv)[:, -1]
    conv_s = cu_s.reshape(db, n_q, d_conv)[:, n_q - (CONV_WIDTH - 1):]
    kv_p = (1, batch, seq, N_HEADS, HEAD_DIM)
    kv_s = (1, db, n_q, N_HEADS, HEAD_DIM)
    return (y_p.reshape(batch, seq, d), y_s.reshape(db, n_q, d),
            k_p.reshape(kv_p), v_p.reshape(kv_p), conv_p[None],
            k_s.reshape(kv_s), v_s.reshape(kv_s), conv_s[None])
```

```python
import functools

import jax
import jax.numpy as jnp
from jax import lax
from jax.experimental import pallas as pl
from jax.experimental.pallas import tpu as pltpu

F32 = jnp.float32
BF16 = jnp.bfloat16

N_HEADS = 8
HEAD_DIM = 128
MOBA_BLOCK = 256
MOBA_TOPK = 3
CONV_WIDTH = 3
NORM_EPS = 1e-6
NEG = float(jnp.finfo(jnp.float32).min)
MASKED = -1e30
ATTN_SCALE = HEAD_DIM ** -0.5
LOG2_E = 1.4426950408889634
SCALE2 = ATTN_SCALE * LOG2_E
Q_TILES = 2

VMEM_LIMIT_BYTES = 56 * 1024 * 1024

_NT = (((1,), (1,)), ((), ()))


def _params(*semantics):
    return pltpu.CompilerParams(dimension_semantics=semantics, vmem_limit_bytes=VMEM_LIMIT_BYTES)


def _ada_kernel(c_ref, w_ref, b_ref, o_ref):
    o_ref[...] = jnp.dot(c_ref[...].astype(BF16), w_ref[...].astype(BF16),
                         preferred_element_type=F32) + b_ref[...]


def _ada(c_all, w_ada, b_ada):
    rows, d = c_all.shape
    n = w_ada.shape[1]
    tn = 1024
    return pl.pallas_call(
        _ada_kernel,
        grid=(n // tn,),
        in_specs=[pl.BlockSpec((rows, d), lambda j: (0, 0)),
                  pl.BlockSpec((d, tn), lambda j: (0, j)),
                  pl.BlockSpec((1, tn), lambda j: (0, j))],
        out_specs=pl.BlockSpec((rows, tn), lambda j: (0, j)),
        out_shape=jax.ShapeDtypeStruct((rows, n), F32),
        compiler_params=_params("arbitrary"),
        name="ada",
    )(c_all, w_ada, b_ada)


def _prep_kernel(x_ref, g_ref, scale_ref, shift_ref, h_ref):
    x = x_ref[...]
    y = x * lax.rsqrt(jnp.mean(x * x, axis=-1, keepdims=True) + NORM_EPS)
    h = (y * g_ref[...]) * (1.0 + scale_ref[...]) + shift_ref[...]
    h_ref[...] = h.astype(BF16)


def _prep(x, g, scale, shift, scale_spec, shift_spec, tm):
    m, d = x.shape
    return pl.pallas_call(
        _prep_kernel,
        grid=(m // tm,),
        in_specs=[pl.BlockSpec((tm, d), lambda i: (i, 0)),
                  pl.BlockSpec((1, d), lambda i: (0, 0)),
                  scale_spec, shift_spec],
        out_specs=pl.BlockSpec((tm, d), lambda i: (i, 0)),
        out_shape=jax.ShapeDtypeStruct((m, d), BF16),
        compiler_params=_params("arbitrary"),
        name="prep",
    )(x, g, scale, shift)


def _proj_kernel(hp_ref, hs_ref, w_ref, op_ref, os_ref, wb_ref):
    @pl.when(pl.program_id(1) == 0)
    def _():
        wb_ref[...] = w_ref[...].astype(BF16)
        os_ref[...] = jnp.dot(hs_ref[...], wb_ref[...], preferred_element_type=F32).astype(os_ref.dtype)

    op_ref[...] = jnp.dot(hp_ref[...], wb_ref[...], preferred_element_type=F32).astype(op_ref.dtype)


def _proj(h_p, h_s, w, col0, ncol, dtype, tm=1024, tn=1024):
    mp, d = h_p.shape
    ms = h_s.shape[0]
    return pl.pallas_call(
        _proj_kernel,
        grid=(ncol, mp // tm),
        in_specs=[pl.BlockSpec((tm, d), lambda j, i: (i, 0)),
                  pl.BlockSpec((ms, d), lambda j, i: (0, 0)),
                  pl.BlockSpec((d, tn), lambda j, i: (0, col0 + j))],
        out_specs=[pl.BlockSpec((None, tm, tn), lambda j, i: (j, i, 0)),
                   pl.BlockSpec((None, ms, tn), lambda j, i: (j, 0, 0))],
        out_shape=[jax.ShapeDtypeStruct((ncol, mp, tn), dtype),
                   jax.ShapeDtypeStruct((ncol, ms, tn), dtype)],
        scratch_shapes=[pltpu.VMEM((d, tn), BF16)],
        compiler_params=_params("arbitrary", "arbitrary"),
        name="proj",
    )(h_p, h_s, w)


def _topk_mask(gate, valid, idx_f, n, axis):
    g = jnp.where(valid, gate, -jnp.inf)
    sel = jnp.zeros(gate.shape, F32)
    for _ in range(MOBA_TOPK):
        m = jnp.max(g, axis=axis, keepdims=True)
        first = jnp.min(jnp.where(g == m, idx_f, float(n)), axis=axis, keepdims=True)
        hit = idx_f == first
        sel = jnp.where(hit & valid, 1.0, sel)
        g = jnp.where(hit, -jnp.inf, g)
    return sel


SUBLANES = 8
ROWS_PER_HEAD = 4
HEAD_PAIR = N_HEADS // 2
PAIRED = N_HEADS // HEAD_PAIR


def _kstream_repack(k_refs, ksum_ref, kb_ref, page_size):
    pages_per_block = MOBA_BLOCK // page_size
    pair_rows = PAIRED * page_size
    row_hh = lax.broadcasted_iota(jnp.int32, (SUBLANES, HEAD_DIM), 0) % PAIRED
    for blk in range(len(k_refs) // pages_per_block):
        for g in range(HEAD_PAIR):
            tot = None
            for j in range(blk * pages_per_block, (blk + 1) * pages_per_block):
                x = k_refs[j][pl.ds(g, pair_rows, stride=HEAD_PAIR), :]
                kb_ref[j * pair_rows:(j + 1) * pair_rows, g * HEAD_DIM:(g + 1) * HEAD_DIM] = (
                    x.astype(BF16))
                part = jnp.sum(x.reshape(pair_rows // SUBLANES, SUBLANES, HEAD_DIM), axis=0)
                tot = part if tot is None else tot + part
            for hh in range(PAIRED):
                head = g + HEAD_PAIR * hh
                ksum_ref[blk:blk + 1, head * HEAD_DIM:(head + 1) * HEAD_DIM] = jnp.sum(
                    jnp.where(row_hh == hh, tot, 0.0), axis=0, keepdims=True)


def _kstream_scores(qpair_ref, kb_ref, st_ref):
    st_ref[...] = lax.dot_general(qpair_ref[...].astype(BF16), kb_ref[...], _NT,
                                  preferred_element_type=F32)


def _attn_p_kernel(pt_ref, slopes_ref, q_ref, k_ref, v_ref, qrows_ref, *refs,
                   n_blocks, n_stream, page_size):
    kp_refs = refs[:n_stream]
    o_ref, st_ref, ksum_ref, ka_ref, vb_ref, kmean_ref, eye_ref, qa_ref, s_ref, kb_ref = refs[n_stream:]
    blk = MOBA_BLOCK
    half = blk // 2
    h = pl.program_id(1)
    group = pl.program_id(2)

    @pl.when(group == 0)
    def _():
        lane = lax.broadcasted_iota(jnp.int32, (blk, HEAD_DIM), 1)
        for n in range(n_blocks):
            rows = slice(n * blk, (n + 1) * blk)
            kf = k_ref[rows, :]
            ka_ref[rows, :HEAD_DIM] = kf.astype(BF16)
            ka_ref[rows, HEAD_DIM:] = jnp.where(lane == n, 1.0, 0.0).astype(BF16)
            vb_ref[rows, :] = v_ref[rows, :].astype(BF16)
            kmean_ref[n:n + 1, :] = jnp.sum(kf, axis=0, keepdims=True) * (1.0 / blk)
        eye_ref[...] = jnp.where(lax.broadcasted_iota(jnp.int32, (blk, blk), 0)
                                 == lax.broadcasted_iota(jnp.int32, (blk, blk), 1),
                                 1.0, 0.0).astype(BF16)

    _kstream_repack(kp_refs, ksum_ref, kb_ref, page_size)

    slope2 = slopes_ref[h] * LOG2_E
    for t in range(Q_TILES):
        qi = group * Q_TILES + t
        qf = q_ref[t * blk:(t + 1) * blk, :]
        gate_t = lax.dot_general(kmean_ref[...], qf, _NT, precision=lax.Precision.HIGHEST,
                                 preferred_element_type=F32)
        row = lax.broadcasted_iota(jnp.int32, gate_t.shape, 0)
        sel_t = _topk_mask(gate_t, row < qi, row.astype(F32), n_blocks, axis=0)
        unpicked_t = jnp.concatenate([jnp.where(sel_t > 0.5, 0.0, MASKED),
                                      jnp.zeros((HEAD_DIM - n_blocks, blk), F32)], axis=0)
        unpicked = lax.dot_general(eye_ref[...], unpicked_t.astype(BF16), _NT,
                                   preferred_element_type=F32)
        qa_ref[t, :, :HEAD_DIM] = qf.astype(BF16)
        qa_ref[t, :, HEAD_DIM:] = unpicked.astype(BF16)
    _kstream_scores(qrows_ref, kb_ref, st_ref)

    def halves(x):
        return x[:, :half], x[:, half:]

    def attend(first_qi):
        key_col = lax.broadcasted_iota(jnp.int32, (1, blk), 1).astype(F32)
        col_bias = slope2 * key_col
        rel = (lax.broadcasted_iota(jnp.int32, (blk, blk), 0)
               - lax.broadcasted_iota(jnp.int32, (blk, blk), 1))
        tiles = [(t, first_qi + t) for t in range(Q_TILES)]

        own, m = {}, {}
        for t, qi in tiles:
            s_own = lax.dot_general(qa_ref[t, :, :HEAD_DIM], ka_ref[qi * blk:(qi + 1) * blk, :HEAD_DIM],
                                    _NT, preferred_element_type=F32) * SCALE2 + col_bias
            own[t] = jnp.where(rel >= 0, s_own, NEG)
            lo, hi = halves(own[t])
            mx = jnp.maximum(lo, hi)
            for n in range(qi):
                bias_n = col_bias + slope2 * float((n - qi) * blk)
                s = lax.dot_general(qa_ref[t], ka_ref[n * blk:(n + 1) * blk, :], _NT,
                                    preferred_element_type=F32) * SCALE2 + bias_n
                s_ref[t, n] = s
                lo, hi = halves(s)
                mx = jnp.maximum(mx, jnp.maximum(lo, hi))
            m[t] = jnp.broadcast_to(jnp.max(mx, axis=1, keepdims=True), (blk, half))

        def probs(s, m_t):
            lo, hi = halves(s)
            lo, hi = jnp.exp2(lo - m_t), jnp.exp2(hi - m_t)
            return lo + hi, jnp.concatenate([lo, hi], axis=1).astype(BF16)

        for t, qi in tiles:
            lsum, p = probs(own[t], m[t])
            acc = jnp.dot(p, vb_ref[qi * blk:(qi + 1) * blk, :], preferred_element_type=F32)
            for n in range(qi):
                part, p = probs(s_ref[t, n], m[t])
                lsum = lsum + part
                acc = acc + jnp.dot(p, vb_ref[n * blk:(n + 1) * blk, :], preferred_element_type=F32)
            l = jnp.sum(lsum, axis=1, keepdims=True)
            o_ref[t * blk:(t + 1) * blk, :] = (acc / l).astype(o_ref.dtype)

    for g in range(n_blocks // Q_TILES):
        pl.when(group == g)(functools.partial(attend, g * Q_TILES))


def _attn_p(q, k, v, slopes, batch, seq, pt, qpair, cache_k, n_pages):
    n_blocks = seq // MOBA_BLOCK
    blk = MOBA_BLOCK
    assert n_blocks % Q_TILES == 0 and n_blocks <= HEAD_DIM
    n_groups = n_blocks // Q_TILES
    db, rows, pair_width = qpair.shape
    width = N_HEADS * HEAD_DIM
    page_size = cache_k.shape[1] // N_HEADS
    n_steps = batch * N_HEADS * n_groups
    n_stream = db * n_pages // n_steps
    assert n_stream * n_steps == db * n_pages and n_pages % n_stream == 0
    assert (n_stream * page_size) % MOBA_BLOCK == 0
    steps_per_sb = n_pages // n_stream
    keys_per_step = n_stream * page_size
    blocks_per_step = keys_per_step // MOBA_BLOCK

    def stream_pos(b, h, qi):
        lin = (b * N_HEADS + h) * n_groups + qi
        return lin // steps_per_sb, lin % steps_per_sb

    def page_spec(j):
        def index(b, h, qi, pt):
            sb, c = stream_pos(b, h, qi)
            return pt[sb * n_pages + c * n_stream + j], 0, 0
        return pl.BlockSpec((None, page_size * N_HEADS, HEAD_DIM), index)

    kv_spec = pl.BlockSpec((seq, HEAD_DIM), lambda b, h, qi, pt: (b, h))
    q_spec = pl.BlockSpec((Q_TILES * blk, HEAD_DIM), lambda b, h, qi, pt: (b * n_groups + qi, h))
    grid_spec = pltpu.PrefetchScalarGridSpec(
        num_scalar_prefetch=1,
        grid=(batch, N_HEADS, n_groups),
        in_specs=[pl.BlockSpec(memory_space=pltpu.SMEM), q_spec, kv_spec, kv_spec,
                  pl.BlockSpec((None, rows, pair_width),
                               lambda b, h, qi, pt: (stream_pos(b, h, qi)[0], 0, 0))]
        + [page_spec(j) for j in range(n_stream)],
        out_specs=[q_spec,
                   pl.BlockSpec((None, rows, PAIRED * keys_per_step),
                                lambda b, h, qi, pt: (stream_pos(b, h, qi)[0], 0, stream_pos(b, h, qi)[1])),
                   pl.BlockSpec((None, None, blocks_per_step, width),
                                lambda b, h, qi, pt: stream_pos(b, h, qi) + (0, 0))],
        scratch_shapes=[pltpu.VMEM((seq, 2 * HEAD_DIM), BF16),
                        pltpu.VMEM((seq, HEAD_DIM), BF16),
                        pltpu.VMEM((n_blocks, HEAD_DIM), F32),
                        pltpu.VMEM((blk, blk), BF16),
                        pltpu.VMEM((Q_TILES, blk, 2 * HEAD_DIM), BF16),
                        pltpu.VMEM((Q_TILES, n_blocks, blk, blk), F32),
                        pltpu.VMEM((PAIRED * keys_per_step, pair_width), BF16)],
    )
    attn, st, ksum = pl.pallas_call(
        functools.partial(_attn_p_kernel, n_blocks=n_blocks, n_stream=n_stream, page_size=page_size),
        grid_spec=grid_spec,
        out_shape=[jax.ShapeDtypeStruct(q.shape, BF16),
                   jax.ShapeDtypeStruct((db, rows, PAIRED * n_pages * page_size), F32),
                   jax.ShapeDtypeStruct((db, steps_per_sb, blocks_per_step, width), F32)],
        compiler_params=_params("arbitrary", "arbitrary", "arbitrary"),
        name="attn_p",
    )(pt, slopes, q, k, v, qpair, *([cache_k] * n_stream))
    return attn, st, ksum.reshape(db, steps_per_sb * blocks_per_step, width)


def _dot_onehot_exact(x, onehot):
    hi = x.astype(BF16)
    rest = x - hi.astype(F32)
    mid = rest.astype(BF16)
    lo = (rest - mid.astype(F32)).astype(BF16)
    return ((jnp.dot(hi, onehot, preferred_element_type=F32)
             + jnp.dot(mid, onehot, preferred_element_type=F32))
            + jnp.dot(lo, onehot, preferred_element_type=F32))


def _select_kernel(qrows_ref, ksum_ref, st_ref, knew_ref, srow_ref,
                   pc_ref, pown_ref, l_ref, ids_ref, *, n_new):
    rows = st_ref.shape[0]
    n_blocks = ksum_ref.shape[0]
    blk = MOBA_BLOCK
    past_len = n_blocks * blk
    pblk = PAIRED * blk
    qr = qrows_ref[...]
    kmean = ksum_ref[...] * (1.0 / blk)
    gate = lax.dot_general(qr, kmean, _NT, precision=lax.Precision.HIGHEST,
                           preferred_element_type=F32)
    lane_f = lax.broadcasted_iota(jnp.int32, gate.shape, 1).astype(F32)
    expand = (lax.broadcasted_iota(jnp.int32, (n_blocks, n_blocks * pblk), 1) // pblk
              == lax.broadcasted_iota(jnp.int32, (n_blocks, n_blocks * pblk), 0)).astype(BF16)
    own_col = (lax.broadcasted_iota(jnp.int32, (rows, pblk), 1) % PAIRED
               == lax.broadcasted_iota(jnp.int32, (rows, pblk), 0) // (ROWS_PER_HEAD * HEAD_PAIR))
    compact = (lax.broadcasted_iota(jnp.int32, (pblk, blk), 0) // PAIRED
               == lax.broadcasted_iota(jnp.int32, (pblk, blk), 1)).astype(BF16)

    slope = srow_ref[:, 0:1]
    qq = lax.broadcasted_iota(jnp.int32, (rows, 1), 0) % ROWS_PER_HEAD
    q_pos = (past_len + qq).astype(F32)
    col = lax.broadcasted_iota(jnp.int32, (1, blk), 1).astype(F32)
    st = st_ref[...]

    g = gate
    firsts, scores = [], []
    for _ in range(MOBA_TOPK):
        m = jnp.max(g, axis=1, keepdims=True)
        first = jnp.min(jnp.where(g == m, lane_f, float(n_blocks)), axis=1, keepdims=True)
        hit = lane_f == first
        g = jnp.where(hit, -jnp.inf, g)
        keep = jnp.dot(jnp.where(hit, 1.0, 0.0).astype(BF16), expand, preferred_element_type=F32)
        kept = st * keep
        sc = kept[:, :pblk]
        for n in range(1, n_blocks):
            sc = sc + kept[:, n * pblk:(n + 1) * pblk]
        sc = _dot_onehot_exact(jnp.where(own_col, sc, 0.0), compact)
        dist = (q_pos - first * float(blk)) - col
        scores.append(sc * ATTN_SCALE - slope * dist)
        firsts.append(first)

    j = lax.broadcasted_iota(jnp.int32, (rows, knew_ref.shape[0]), 1)
    so = lax.dot_general(qr.astype(BF16), knew_ref[...].astype(BF16), _NT,
                         preferred_element_type=F32) * ATTN_SCALE
    so = so - slope * (qq - j).astype(F32)
    so = jnp.where((j <= qq) & (j < n_new), so, NEG)

    m = jnp.max(so, axis=1, keepdims=True)
    for s in scores:
        m = jnp.maximum(m, jnp.max(s, axis=1, keepdims=True))
    po = jnp.exp(so - m)
    l = jnp.sum(po, axis=1, keepdims=True)
    for t, s in enumerate(scores):
        p = jnp.exp(s - m)
        l = l + jnp.sum(p, axis=1, keepdims=True)
        pc_ref[:, t * blk:(t + 1) * blk] = p
    pown_ref[...] = po
    l_ref[...] = jnp.broadcast_to(l, l_ref.shape)
    out_lane = lax.broadcasted_iota(jnp.int32, ids_ref.shape, 1)
    ids = jnp.zeros(ids_ref.shape, F32)
    for t, first in enumerate(firsts):
        ids = jnp.where(out_lane == t, first, ids)
    ids_ref[...] = ids.astype(jnp.int32)


def _select(qrows, ksum, st, knew, srow, n_new):
    db, rows, width = qrows.shape
    n_blocks = ksum.shape[1]
    past_len = st.shape[2]
    pad = knew.shape[1]
    assert n_blocks >= MOBA_TOPK and past_len == PAIRED * n_blocks * MOBA_BLOCK
    per_b = lambda *shape: pl.BlockSpec((None,) + shape, lambda b: (b,) + (0,) * len(shape))
    return pl.pallas_call(
        functools.partial(_select_kernel, n_new=n_new),
        grid=(db,),
        in_specs=[per_b(rows, width), per_b(n_blocks, width), per_b(rows, past_len),
                  per_b(pad, width), pl.BlockSpec((rows, HEAD_DIM), lambda b: (0, 0))],
        out_specs=[per_b(rows, MOBA_TOPK * MOBA_BLOCK), per_b(rows, pad), per_b(rows, HEAD_DIM),
                   per_b(rows, HEAD_DIM)],
        out_shape=[jax.ShapeDtypeStruct((db, rows, MOBA_TOPK * MOBA_BLOCK), F32),
                   jax.ShapeDtypeStruct((db, rows, pad), F32),
                   jax.ShapeDtypeStruct((db, rows, HEAD_DIM), F32),
                   jax.ShapeDtypeStruct((db, rows, HEAD_DIM), jnp.int32)],
        compiler_params=_params("arbitrary"),
        name="select",
    )(qrows, ksum, st, knew, srow)


def _gather_copy(picks_ref, pt_ref, cache_ref, v_buf, sem, sb, h, s, *, n_slots, n_pages, page_size):
    pages_per_block = MOBA_BLOCK // page_size
    block = picks_ref[(sb * N_HEADS + h) * (n_slots // pages_per_block) + s // pages_per_block]
    page = pt_ref[sb * n_pages + block * pages_per_block + s % pages_per_block]
    return pltpu.make_async_copy(cache_ref.at[page, :, h, :], v_buf.at[h, s], sem.at[h])


def _gather_attend(h, pc_ref, pown_ref, l_ref, vnew_ref, v_buf, o_ref, *, n_q, n_slots):
    page_size = v_buf.shape[2]
    heads_per_tile = SUBLANES // ROWS_PER_HEAD
    tile_rows = slice((h // heads_per_tile) * SUBLANES, (h // heads_per_tile + 1) * SUBLANES)
    base = (h % heads_per_tile) * ROWS_PER_HEAD
    cols = slice(h * HEAD_DIM, (h + 1) * HEAD_DIM)
    pc = pc_ref[tile_rows, :]
    row = lax.broadcasted_iota(jnp.int32, (SUBLANES, page_size), 0)
    acc = jnp.dot(pown_ref[tile_rows, :].astype(BF16), vnew_ref[:, cols].astype(BF16),
                  preferred_element_type=F32)
    slots_per_q = n_slots // n_q
    for q in range(n_q):
        for t in range(slots_per_q):
            p = jnp.where(row == base + q, pc[:, t * page_size:(t + 1) * page_size], 0.0)
            v = v_buf[h, q * slots_per_q + t].astype(BF16)
            acc = acc + jnp.dot(p.astype(BF16), v, preferred_element_type=F32)
    out = acc / l_ref[tile_rows, 0:1]
    o_ref[:, cols] = out[base:base + n_q, :]


def _sample_query_rows(q_s, db, n_q):
    assert n_q <= ROWS_PER_HEAD
    rows = ROWS_PER_HEAD * N_HEADS
    q4 = jnp.pad(q_s.reshape(db, n_q, N_HEADS, HEAD_DIM).transpose(0, 2, 1, 3),
                 ((0, 0), (0, 0), (0, ROWS_PER_HEAD - n_q), (0, 0)))
    place = lambda onehot: (q4[:, :, :, None, :] * onehot[None, :, None, :, None]
                            ).reshape(db, rows, onehot.shape[1] * HEAD_DIM)
    heads = jnp.arange(N_HEADS)
    own = (heads[:, None] == heads[None, :]).astype(F32)
    pair = (heads[:, None] % HEAD_PAIR == jnp.arange(HEAD_PAIR)[None, :]).astype(F32)
    return place(own), place(pair)


def _sample_probs(qrows, st, ksum, k_s, v_s, slopes, db, n_q):
    d_attn = N_HEADS * HEAD_DIM
    rows = ROWS_PER_HEAD * N_HEADS
    pad_new = ((0, 0), (0, HEAD_DIM - n_q), (0, 0))
    knew = jnp.pad(k_s.reshape(db, n_q, d_attn), pad_new)
    vnew = jnp.pad(v_s.reshape(db, n_q, d_attn), pad_new)
    srow = jnp.broadcast_to(jnp.repeat(slopes, ROWS_PER_HEAD)[:, None], (rows, HEAD_DIM))
    pc, pown, l, ids = _select(qrows, ksum, st, knew, srow, n_q)
    picks = ids[:, :, :MOBA_TOPK].reshape(db, N_HEADS, ROWS_PER_HEAD, MOBA_TOPK)[:, :, :n_q]
    return picks.reshape(-1), pc, pown, l, vnew


HALO_ROWS = 16


N_OUT_INPUTS = 19


def _out_kernel(*refs, sample, tiles_per_seq, n_q, gather):
    if gather is not None:
        picks_ref, pt_ref = refs[:2]
        refs = refs[2:]
        pc_ref, pown_ref, l_ref, vnew_ref, cache_ref = refs[N_OUT_INPUTS:N_OUT_INPUTS + 5]
        y_ref, cu_ref, attn_s_ref, v_buf, sem = refs[N_OUT_INPUTS + 5:]
        sb = pl.program_id(0)
        copy = functools.partial(_gather_copy, picks_ref, pt_ref, cache_ref, v_buf, sem,
                                 n_slots=gather["n_slots"], n_pages=gather["n_pages"],
                                 page_size=gather["page_size"])

        @pl.when(sb == 0)
        def _():
            for h in range(N_HEADS):
                for s in range(gather["n_slots"]):
                    copy(sb, h, s).start()
    else:
        y_ref, cu_ref = refs[N_OUT_INPUTS:]
    (attn_ref, za_ref, bg_ref, cg_ref, u_ref, zb_ref, ga0_ref, ga1_ref, gb0_ref, gb1_ref,
     h1_ref, h2_ref, x_ref, gate_ref, cw_ref, wpa_ref, wpb_ref, wo_ref, fg_ref) = refs[:N_OUT_INPUTS]
    tm = x_ref.shape[0]
    cu = cg_ref[...].astype(F32) * u_ref[...].astype(F32)
    row = lax.broadcasted_iota(jnp.int32, cu.shape, 0)
    prev1 = pltpu.roll(cu, 1, 0)
    prev2 = pltpu.roll(cu, 2, 0)
    if sample:
        t = row % n_q
        prev1 = jnp.where(t >= 1, prev1, h1_ref[...])
        prev2 = jnp.where(t >= 2, prev2, h2_ref[...])
        cu_ref[...] = cu
    else:
        first = (pl.program_id(0) % tiles_per_seq) == 0
        halo = h1_ref[...].astype(F32) * h2_ref[...].astype(F32)
        halo = jnp.where(first, 0.0, halo)
        before1 = halo[HALO_ROWS - 1:HALO_ROWS, :]
        before2 = halo[HALO_ROWS - 2:HALO_ROWS - 1, :]
        prev1 = jnp.where(row >= 1, prev1, before1)
        prev2 = jnp.where(row >= 2, prev2, jnp.where(row == 0, before2, before1))
        cu_ref[...] = cu[tm - (CONV_WIDTH - 1):, :]

    conv = bg_ref[...].astype(F32) * (cw_ref[0:1, :] * prev2 + cw_ref[1:2, :] * prev1
                                      + cw_ref[2:3, :] * cu)
    a_act = (attn_ref[...].astype(F32) * jax.nn.silu(za_ref[...].astype(F32))).astype(BF16)
    b_act = (conv * jax.nn.silu(zb_ref[...].astype(F32))).astype(BF16)
    ya = jnp.dot(a_act, wpa_ref[...], preferred_element_type=F32)
    yb = jnp.dot(b_act, wpb_ref[...], preferred_element_type=F32)
    ga = jnp.concatenate([ga0_ref[...], ga1_ref[...]], axis=1).astype(F32)
    gb = jnp.concatenate([gb0_ref[...], gb1_ref[...]], axis=1).astype(F32)
    merged = jax.nn.sigmoid(ga) * ya + jax.nn.sigmoid(gb) * yb
    out = jnp.dot(merged.astype(BF16), wo_ref[...], preferred_element_type=F32)
    xn = x_ref[...] + gate_ref[...] * out
    y = xn * lax.rsqrt(jnp.mean(xn * xn, axis=-1, keepdims=True) + NORM_EPS)
    y_ref[...] = y * fg_ref[...]

    if gather is not None:
        for h in range(N_HEADS):
            for s in range(gather["n_slots"]):
                copy(sb, h, s).wait()
            _gather_attend(h, pc_ref, pown_ref, l_ref, vnew_ref, v_buf, attn_s_ref,
                           n_q=gather["n_q"], n_slots=gather["n_slots"])

            @pl.when(sb + 1 < pl.num_programs(0))
            def _():
                for s in range(gather["n_slots"]):
                    copy(sb + 1, h, s).start()


def _out(attn, rest, h1, h2, x, gate, gate_spec, conv_w, w_pa, w_pb, w_o, final_g,
         *, sample, tm, tiles_per_seq=1, n_q=1, gather=None):
    m, d = x.shape
    dc = attn.shape[1]
    tiles = m // tm
    slab = lambda s: pl.BlockSpec((None, tm, dc), lambda i, *_: (s, i, 0))
    const = lambda shape: pl.BlockSpec(shape, lambda i, *_: (0,) * len(shape),
                                       pipeline_mode=pl.Buffered(1))
    if sample:
        hist_specs = [pl.BlockSpec((tm, dc), lambda i, *_: (i, 0))] * 2
        cu_spec = pl.BlockSpec((tm, dc), lambda i, *_: (i, 0))
        cu_shape = jax.ShapeDtypeStruct((m, dc), F32)
    else:
        per_tile = tm // HALO_ROWS
        halo = lambda s: pl.BlockSpec((None, HALO_ROWS, dc),
                                      lambda i, *_: (s, jnp.maximum(i * per_tile - 1, 0), 0))
        hist_specs = [halo(2), halo(3)]
        cu_spec = pl.BlockSpec((None, CONV_WIDTH - 1, dc), lambda i, *_: (i, 0, 0))
        cu_shape = jax.ShapeDtypeStruct((tiles, CONV_WIDTH - 1, dc), F32)
    in_specs = ([pl.BlockSpec((tm, dc), lambda i, *_: (i, 0))]
                + [slab(s) for s in (0, 1, 2, 3, 4, 5, 6, 7, 8)]
                + hist_specs
                + [pl.BlockSpec((tm, d), lambda i, *_: (i, 0)), gate_spec,
                   const(conv_w.shape), const(w_pa.shape), const(w_pb.shape), const(w_o.shape),
                   const(final_g.shape)])
    operands = [attn, *([rest] * 9), h1, h2, x, gate, conv_w, w_pa, w_pb, w_o, final_g]
    assert len(operands) == N_OUT_INPUTS
    out_specs = [pl.BlockSpec((tm, d), lambda i, *_: (i, 0)), cu_spec]
    out_shape = [jax.ShapeDtypeStruct((m, d), F32), cu_shape]
    scratch, prefetch, static = [], [], None
    if gather is not None:
        picks, pt, pc, pown, l, vnew, cache_v, gather_q, n_pages = gather
        db, rows, _ = pc.shape
        page_size = cache_v.shape[1]
        n_slots = gather_q * MOBA_TOPK * (MOBA_BLOCK // page_size)
        assert tiles == db, "one sample batch rides on each row tile"
        per_b = lambda a: pl.BlockSpec((None,) + a.shape[1:], lambda i, *_: (i, 0, 0))
        in_specs += [per_b(pc), per_b(pown), per_b(l), per_b(vnew), pl.BlockSpec(memory_space=pl.ANY)]
        operands += [pc, pown, l, vnew, cache_v]
        out_specs.append(pl.BlockSpec((None, gather_q, dc), lambda i, *_: (i, 0, 0)))
        out_shape.append(jax.ShapeDtypeStruct((db, gather_q, dc), F32))
        scratch = [pltpu.VMEM((N_HEADS, n_slots, page_size, HEAD_DIM), F32),
                   pltpu.SemaphoreType.DMA((N_HEADS,))]
        prefetch = [picks, pt]
        static = dict(n_slots=n_slots, n_pages=n_pages, page_size=page_size, n_q=gather_q)
    grid_spec = pltpu.PrefetchScalarGridSpec(
        num_scalar_prefetch=len(prefetch), grid=(tiles,), in_specs=in_specs, out_specs=out_specs,
        scratch_shapes=scratch)
    return pl.pallas_call(
        functools.partial(_out_kernel, sample=sample, tiles_per_seq=tiles_per_seq, n_q=n_q,
                          gather=static),
        grid_spec=grid_spec,
        out_shape=out_shape,
        compiler_params=_params("arbitrary"),
        name="out",
    )(*prefetch, *operands)


def kernel(x_prompt, x_sample, cache_k, cache_v, state_conv, page_table, c_prompt, c_sample,
           norm_g, w_ada, b_ada, w_in, conv_w, w_pa, w_pb, w_o, final_g):
    depth = norm_g.shape[0]
    assert depth == 1, "single-layer step only"
    batch, seq, d = x_prompt.shape
    db, n_q, _ = x_sample.shape
    n_pool, page_size, n_heads, head_dim = cache_k.shape[1:]
    assert (n_heads, head_dim) == (N_HEADS, HEAD_DIM)
    n_pages = page_table.shape[1]
    past_len = n_pages * page_size
    d_attn = n_heads * head_dim
    d_conv = d - d_attn
    assert d_attn == d_conv == 1024 and w_in.shape[2] == 12 * 1024
    assert seq % MOBA_BLOCK == 0 and past_len % MOBA_BLOCK == 0 and n_q <= MOBA_BLOCK

    slopes = jnp.exp2(-8.0 * jnp.arange(1, N_HEADS + 1, dtype=F32) / N_HEADS)

    n_c = batch + db
    c_all = jnp.pad(jnp.concatenate([c_prompt, c_sample], axis=0), ((0, -n_c % 8), (0, 0)))
    mod = _ada(c_all, w_ada[0], b_ada[0][None, :])
    mod4 = mod.reshape(mod.shape[0], 3, 1, d)
    mod_tok = jnp.repeat(mod[batch:n_c].reshape(db, 3, d), n_q, axis=0)

    xp = x_prompt.reshape(batch * seq, d)
    xs = x_sample.reshape(db * n_q, d)
    g = norm_g[0][None, :]
    tm_prep = 512
    per_seq = seq // tm_prep
    mod_spec = lambda s: pl.BlockSpec((None, None, 1, d), lambda i: (i // per_seq, s, 0, 0))
    h_p = _prep(xp, g, mod4, mod4, mod_spec(1), mod_spec(0), tm_prep)
    tok_spec = pl.BlockSpec((db * n_q, d), lambda i: (0, 0))
    h_s = _prep(xs, g, mod_tok[:, 1], mod_tok[:, 0], tok_spec, tok_spec, db * n_q)

    w = w_in[0]
    (q_p,), (q_s,) = _proj(h_p, h_s, w, 0, 1, F32)
    (k_p,), (k_s,) = _proj(h_p, h_s, w, 1, 1, F32)
    (v_p,), (v_s,) = _proj(h_p, h_s, w, 2, 1, F32)
    rest_p, rest_s = _proj(h_p, h_s, w, 3, 9, BF16)

    ck = cache_k[0].reshape(n_pool, page_size * N_HEADS, HEAD_DIM)
    pt = page_table.reshape(-1).astype(jnp.int32)
    qrows, qpair = _sample_query_rows(q_s, db, n_q)
    attn_p, st, ksum = _attn_p(q_p, k_p, v_p, slopes, batch, seq, pt, qpair, ck, n_pages)
    picks, pc, pown, l, vnew = _sample_probs(qrows, st, ksum, k_s, v_s, slopes, db, n_q)

    cw = conv_w[0]
    wpa, wpb, wo = w_pa[0].astype(BF16), w_pb[0].astype(BF16), w_o[0].astype(BF16)
    fg = final_g[None, :]
    tm_out = batch * seq // db
    assert seq % tm_out == 0 and tm_out % HALO_ROWS == 0
    tiles_per_seq = seq // tm_out
    gate_p_spec = pl.BlockSpec((None, None, 1, d), lambda i, *_: (i // tiles_per_seq, 2, 0, 0))
    y_p, cu_tail, attn_s = _out(
        attn_p, rest_p, rest_p, rest_p, xp, mod4, gate_p_spec, cw, wpa, wpb, wo, fg,
        sample=False, tm=tm_out, tiles_per_seq=tiles_per_seq,
        gather=(picks, pt, pc, pown, l, vnew, cache_v[0], n_q, n_pages))
    attn_s = attn_s.reshape(db * n_q, d_attn)

    state = state_conv[0]
    zeros = jnp.zeros((db, n_q - 1, d_conv), F32)
    hist1 = jnp.concatenate([state[:, 1:2], zeros], axis=1).reshape(db * n_q, d_conv)
    hist2 = jnp.concatenate([state, zeros[:, 1:]], axis=1).reshape(db * n_q, d_conv)
    gate_s_spec = pl.BlockSpec((db * n_q, d), lambda i, *_: (0, 0))
    y_s, cu_s = _out(attn_s, rest_s, hist1, hist2, xs, mod_tok[:, 2], gate_s_spec, cw, wpa, wpb, wo, fg,
                     sample=True, tm=db * n_q, n_q=n_q)

    conv_p = cu_tail.reshape(batch, tiles_per_seq, CONV_WIDTH - 1, d_conv)[:, -1]
    conv_s = cu_s.reshape(db, n_q, d_conv)[:, n_q - (CONV_WIDTH - 1):]
    kv_p = (1, batch, seq, N_HEADS, HEAD_DIM)
    kv_s = (1, db, n_q, N_HEADS, HEAD_DIM)
    return (y_p.reshape(batch, seq, d), y_s.reshape(db, n_q, d),
            k_p.reshape(kv_p), v_p.reshape(kv_p), conv_p[None],
            k_s.reshape(kv_s), v_s.reshape(kv_s), conv_s[None])
```

```python
import functools

import jax
import jax.numpy as jnp
from jax import lax
from jax.experimental import pallas as pl
from jax.experimental.pallas import tpu as pltpu

F32 = jnp.float32
BF16 = jnp.bfloat16

N_HEADS = 8
HEAD_DIM = 128
MOBA_BLOCK = 256
MOBA_TOPK = 3
CONV_WIDTH = 3
NORM_EPS = 1e-6
NEG = float(jnp.finfo(jnp.float32).min)
MASKED = -1e30
ATTN_SCALE = HEAD_DIM ** -0.5
LOG2_E = 1.4426950408889634
SCALE2 = ATTN_SCALE * LOG2_E
Q_TILES = 2

VMEM_LIMIT_BYTES = 56 * 1024 * 1024

_NT = (((1,), (1,)), ((), ()))


def _params(*semantics):
    return pltpu.CompilerParams(dimension_semantics=semantics, vmem_limit_bytes=VMEM_LIMIT_BYTES)


def _ada_kernel(c_ref, w_ref, b_ref, o_ref):
    o_ref[...] = jnp.dot(c_ref[...].astype(BF16), w_ref[...].astype(BF16),
                         preferred_element_type=F32) + b_ref[...]


def _ada(c_all, w_ada, b_ada):
    rows, d = c_all.shape
    n = w_ada.shape[1]
    tn = 1024
    return pl.pallas_call(
        _ada_kernel,
        grid=(n // tn,),
        in_specs=[pl.BlockSpec((rows, d), lambda j: (0, 0)),
                  pl.BlockSpec((d, tn), lambda j: (0, j)),
                  pl.BlockSpec((1, tn), lambda j: (0, j))],
        out_specs=pl.BlockSpec((rows, tn), lambda j: (0, j)),
        out_shape=jax.ShapeDtypeStruct((rows, n), F32),
        compiler_params=_params("arbitrary"),
        name="ada",
    )(c_all, w_ada, b_ada)


def _prep_kernel(x_ref, g_ref, scale_ref, shift_ref, h_ref):
    x = x_ref[...]
    y = x * lax.rsqrt(jnp.mean(x * x, axis=-1, keepdims=True) + NORM_EPS)
    h = (y * g_ref[...]) * (1.0 + scale_ref[...]) + shift_ref[...]
    h_ref[...] = h.astype(BF16)


def _prep(x, g, scale, shift, scale_spec, shift_spec, tm):
    m, d = x.shape
    return pl.pallas_call(
        _prep_kernel,
        grid=(m // tm,),
        in_specs=[pl.BlockSpec((tm, d), lambda i: (i, 0)),
                  pl.BlockSpec((1, d), lambda i: (0, 0)),
                  scale_spec, shift_spec],
        out_specs=pl.BlockSpec((tm, d), lambda i: (i, 0)),
        out_shape=jax.ShapeDtypeStruct((m, d), BF16),
        compiler_params=_params("arbitrary"),
        name="prep",
    )(x, g, scale, shift)


def _proj_kernel(hp_ref, hs_ref, w_ref, op_ref, os_ref, wb_ref):
    @pl.when(pl.program_id(1) == 0)
    def _():
        wb_ref[...] = w_ref[...].astype(BF16)
        os_ref[...] = jnp.dot(hs_ref[...], wb_ref[...], preferred_element_type=F32).astype(os_ref.dtype)

    op_ref[...] = jnp.dot(hp_ref[...], wb_ref[...], preferred_element_type=F32).astype(op_ref.dtype)


def _proj(h_p, h_s, w, col0, ncol, dtype, tm=1024, tn=1024):
    mp, d = h_p.shape
    ms = h_s.shape[0]
    return pl.pallas_call(
        _proj_kernel,
        grid=(ncol, mp // tm),
        in_specs=[pl.BlockSpec((tm, d), lambda j, i: (i, 0)),
                  pl.BlockSpec((ms, d), lambda j, i: (0, 0)),
                  pl.BlockSpec((d, tn), lambda j, i: (0, col0 + j))],
        out_specs=[pl.BlockSpec((None, tm, tn), lambda j, i: (j, i, 0)),
                   pl.BlockSpec((None, ms, tn), lambda j, i: (j, 0, 0))],
        out_shape=[jax.ShapeDtypeStruct((ncol, mp, tn), dtype),
                   jax.ShapeDtypeStruct((ncol, ms, tn), dtype)],
        scratch_shapes=[pltpu.VMEM((d, tn), BF16)],
        compiler_params=_params("arbitrary", "arbitrary"),
        name="proj",
    )(h_p, h_s, w)


def _topk_mask(gate, valid, idx_f, n, axis):
    g = jnp.where(valid, gate, -jnp.inf)
    sel = jnp.zeros(gate.shape, F32)
    for _ in range(MOBA_TOPK):
        m = jnp.max(g, axis=axis, keepdims=True)
        first = jnp.min(jnp.where(g == m, idx_f, float(n)), axis=axis, keepdims=True)
        hit = idx_f == first
        sel = jnp.where(hit & valid, 1.0, sel)
        g = jnp.where(hit, -jnp.inf, g)
    return sel


SUBLANES = 8
ROWS_PER_HEAD = 4
HEAD_PAIR = N_HEADS // 2
PAIRED = N_HEADS // HEAD_PAIR


def _kstream_repack(k_refs, ksum_ref, kb_ref, page_size):
    pages_per_block = MOBA_BLOCK // page_size
    pair_rows = PAIRED * page_size
    row_hh = lax.broadcasted_iota(jnp.int32, (SUBLANES, HEAD_DIM), 0) % PAIRED
    for blk in range(len(k_refs) // pages_per_block):
        for g in range(HEAD_PAIR):
            tot = None
            for j in range(blk * pages_per_block, (blk + 1) * pages_per_block):
                x = k_refs[j][pl.ds(g, pair_rows, stride=HEAD_PAIR), :]
                kb_ref[j * pair_rows:(j + 1) * pair_rows, g * HEAD_DIM:(g + 1) * HEAD_DIM] = (
                    x.astype(BF16))
                part = jnp.sum(x.reshape(pair_rows // SUBLANES, SUBLANES, HEAD_DIM), axis=0)
                tot = part if tot is None else tot + part
            for hh in range(PAIRED):
                head = g + HEAD_PAIR * hh
                ksum_ref[blk:blk + 1, head * HEAD_DIM:(head + 1) * HEAD_DIM] = jnp.sum(
                    jnp.where(row_hh == hh, tot, 0.0), axis=0, keepdims=True)


def _kstream_scores(qpair_ref, kb_ref, st_ref):
    st_ref[...] = lax.dot_general(qpair_ref[...].astype(BF16), kb_ref[...], _NT,
                                  preferred_element_type=F32)


def _attn_p_kernel(pt_ref, slopes_ref, q_ref, k_ref, v_ref, qrows_ref, *refs,
                   n_blocks, n_stream, page_size):
    kp_refs = refs[:n_stream]
    o_ref, st_ref, ksum_ref, ka_ref, vb_ref, kmean_ref, eye_ref, qa_ref, s_ref, kb_ref = refs[n_stream:]
    blk = MOBA_BLOCK
    half = blk // 2
    h = pl.program_id(1)
    group = pl.program_id(2)

    @pl.when(group == 0)
    def _():
        lane = lax.broadcasted_iota(jnp.int32, (blk, HEAD_DIM), 1)
        for n in range(n_blocks):
            rows = slice(n * blk, (n + 1) * blk)
            kf = k_ref[rows, :]
            ka_ref[rows, :HEAD_DIM] = kf.astype(BF16)
            ka_ref[rows, HEAD_DIM:] = jnp.where(lane == n, 1.0, 0.0).astype(BF16)
            vb_ref[rows, :] = v_ref[rows, :].astype(BF16)
            kmean_ref[n:n + 1, :] = jnp.sum(kf, axis=0, keepdims=True) * (1.0 / blk)
        eye_ref[...] = jnp.where(lax.broadcasted_iota(jnp.int32, (blk, blk), 0)
                                 == lax.broadcasted_iota(jnp.int32, (blk, blk), 1),
                                 1.0, 0.0).astype(BF16)

    _kstream_repack(kp_refs, ksum_ref, kb_ref, page_size)

    slope2 = slopes_ref[h] * LOG2_E
    for t in range(Q_TILES):
        qi = group * Q_TILES + t
        qf = q_ref[t * blk:(t + 1) * blk, :]
        gate_t = lax.dot_general(kmean_ref[...], qf, _NT, precision=lax.Precision.HIGHEST,
                                 preferred_element_type=F32)
        row = lax.broadcasted_iota(jnp.int32, gate_t.shape, 0)
        sel_t = _topk_mask(gate_t, row < qi, row.astype(F32), n_blocks, axis=0)
        unpicked_t = jnp.concatenate([jnp.where(sel_t > 0.5, 0.0, MASKED),
                                      jnp.zeros((HEAD_DIM - n_blocks, blk), F32)], axis=0)
        unpicked = lax.dot_general(eye_ref[...], unpicked_t.astype(BF16), _NT,
                                   preferred_element_type=F32)
        qa_ref[t, :, :HEAD_DIM] = qf.astype(BF16)
        qa_ref[t, :, HEAD_DIM:] = unpicked.astype(BF16)
    _kstream_scores(qrows_ref, kb_ref, st_ref)

    def halves(x):
        return x[:, :half], x[:, half:]

    def attend(first_qi):
        key_col = lax.broadcasted_iota(jnp.int32, (1, blk), 1).astype(F32)
        col_bias = slope2 * key_col
        rel = (lax.broadcasted_iota(jnp.int32, (blk, blk), 0)
               - lax.broadcasted_iota(jnp.int32, (blk, blk), 1))
        tiles = [(t, first_qi + t) for t in range(Q_TILES)]

        own, m = {}, {}
        for t, qi in tiles:
            s_own = lax.dot_general(qa_ref[t, :, :HEAD_DIM], ka_ref[qi * blk:(qi + 1) * blk, :HEAD_DIM],
                                    _NT, preferred_element_type=F32) * SCALE2 + col_bias
            own[t] = jnp.where(rel >= 0, s_own, NEG)
            lo, hi = halves(own[t])
            mx = jnp.maximum(lo, hi)
            for n in range(qi):
                bias_n = col_bias + slope2 * float((n - qi) * blk)
                s = lax.dot_general(qa_ref[t], ka_ref[n * blk:(n + 1) * blk, :], _NT,
                                    preferred_element_type=F32) * SCALE2 + bias_n
                s_ref[t, n] = s
                lo, hi = halves(s)
                mx = jnp.maximum(mx, jnp.maximum(lo, hi))
            m[t] = jnp.broadcast_to(jnp.max(mx, axis=1, keepdims=True), (blk, half))

        def probs(s, m_t):
            lo, hi = halves(s)
            lo, hi = jnp.exp2(lo - m_t), jnp.exp2(hi - m_t)
            return lo + hi, jnp.concatenate([lo, hi], axis=1).astype(BF16)

        for t, qi in tiles:
            lsum, p = probs(own[t], m[t])
            acc = jnp.dot(p, vb_ref[qi * blk:(qi + 1) * blk, :], preferred_element_type=F32)
            for n in range(qi):
                part, p = probs(s_ref[t, n], m[t])
                lsum = lsum + part
                acc = acc + jnp.dot(p, vb_ref[n * blk:(n + 1) * blk, :], preferred_element_type=F32)
            l = jnp.sum(lsum, axis=1, keepdims=True)
            o_ref[t * blk:(t + 1) * blk, :] = (acc / l).astype(o_ref.dtype)

    for g in range(n_blocks // Q_TILES):
        pl.when(group == g)(functools.partial(attend, g * Q_TILES))


def _attn_p(q, k, v, slopes, batch, seq, pt, qpair, cache_k, n_pages):
    n_blocks = seq // MOBA_BLOCK
    blk = MOBA_BLOCK
    assert n_blocks % Q_TILES == 0 and n_blocks <= HEAD_DIM
    n_groups = n_blocks // Q_TILES
    db, rows, pair_width = qpair.shape
    width = N_HEADS * HEAD_DIM
    page_size = cache_k.shape[1] // N_HEADS
    n_steps = batch * N_HEADS * n_groups
    n_stream = db * n_pages // n_steps
    assert n_stream * n_steps == db * n_pages and n_pages % n_stream == 0
    assert (n_stream * page_size) % MOBA_BLOCK == 0
    steps_per_sb = n_pages // n_stream
    keys_per_step = n_stream * page_size
    blocks_per_step = keys_per_step // MOBA_BLOCK

    def stream_pos(b, h, qi):
        lin = (b * N_HEADS + h) * n_groups + qi
        return lin // steps_per_sb, lin % steps_per_sb

    def page_spec(j):
        def index(b, h, qi, pt):
            sb, c = stream_pos(b, h, qi)
            return pt[sb * n_pages + c * n_stream + j], 0, 0
        return pl.BlockSpec((None, page_size * N_HEADS, HEAD_DIM), index)

    kv_spec = pl.BlockSpec((seq, HEAD_DIM), lambda b, h, qi, pt: (b, h))
    q_spec = pl.BlockSpec((Q_TILES * blk, HEAD_DIM), lambda b, h, qi, pt: (b * n_groups + qi, h))
    grid_spec = pltpu.PrefetchScalarGridSpec(
        num_scalar_prefetch=1,
        grid=(batch, N_HEADS, n_groups),
        in_specs=[pl.BlockSpec(memory_space=pltpu.SMEM), q_spec, kv_spec, kv_spec,
                  pl.BlockSpec((None, rows, pair_width),
                               lambda b, h, qi, pt: (stream_pos(b, h, qi)[0], 0, 0))]
        + [page_spec(j) for j in range(n_stream)],
        out_specs=[q_spec,
                   pl.BlockSpec((None, rows, PAIRED * keys_per_step),
                                lambda b, h, qi, pt: (stream_pos(b, h, qi)[0], 0, stream_pos(b, h, qi)[1])),
                   pl.BlockSpec((None, None, blocks_per_step, width),
                                lambda b, h, qi, pt: stream_pos(b, h, qi) + (0, 0))],
        scratch_shapes=[pltpu.VMEM((seq, 2 * HEAD_DIM), BF16),
                        pltpu.VMEM((seq, HEAD_DIM), BF16),
                        pltpu.VMEM((n_blocks, HEAD_DIM), F32),
                        pltpu.VMEM((blk, blk), BF16),
                        pltpu.VMEM((Q_TILES, blk, 2 * HEAD_DIM), BF16),
                        pltpu.VMEM((Q_TILES, n_blocks, blk, blk), F32),
                        pltpu.VMEM((PAIRED * keys_per_step, pair_width), BF16)],
    )
    attn, st, ksum = pl.pallas_call(
        functools.partial(_attn_p_kernel, n_blocks=n_blocks, n_stream=n_stream, page_size=page_size),
        grid_spec=grid_spec,
        out_shape=[jax.ShapeDtypeStruct(q.shape, BF16),
                   jax.ShapeDtypeStruct((db, rows, PAIRED * n_pages * page_size), F32),
                   jax.ShapeDtypeStruct((db, steps_per_sb, blocks_per_step, width), F32)],
        compiler_params=_params("arbitrary", "arbitrary", "arbitrary"),
        name="attn_p",
    )(pt, slopes, q, k, v, qpair, *([cache_k] * n_stream))
    return attn, st, ksum.reshape(db, steps_per_sb * blocks_per_step, width)


def _dot_onehot_exact(x, onehot):
    hi = x.astype(BF16)
    rest = x - hi.astype(F32)
    mid = rest.astype(BF16)
    lo = (rest - mid.astype(F32)).astype(BF16)
    return ((jnp.dot(hi, onehot, preferred_element_type=F32)
             + jnp.dot(mid, onehot, preferred_element_type=F32))
            + jnp.dot(lo, onehot, preferred_element_type=F32))


def _select_kernel(qrows_ref, ksum_ref, st_ref, knew_ref, srow_ref,
                   pc_ref, pown_ref, l_ref, ids_ref, *, n_new):
    rows = st_ref.shape[0]
    n_blocks = ksum_ref.shape[0]
    blk = MOBA_BLOCK
    past_len = n_blocks * blk
    pblk = PAIRED * blk
    qr = qrows_ref[...]
    kmean = ksum_ref[...] * (1.0 / blk)
    gate = lax.dot_general(qr, kmean, _NT, precision=lax.Precision.HIGHEST,
                           preferred_element_type=F32)
    lane_f = lax.broadcasted_iota(jnp.int32, gate.shape, 1).astype(F32)
    expand = (lax.broadcasted_iota(jnp.int32, (n_blocks, n_blocks * HEAD_DIM), 1) // HEAD_DIM
              == lax.broadcasted_iota(jnp.int32, (n_blocks, n_blocks * HEAD_DIM), 0)).astype(BF16)
    own_col = (lax.broadcasted_iota(jnp.int32, (rows, pblk), 1) % PAIRED
               == lax.broadcasted_iota(jnp.int32, (rows, pblk), 0) // (ROWS_PER_HEAD * HEAD_PAIR))
    compact = (lax.broadcasted_iota(jnp.int32, (pblk, blk), 0) // PAIRED
               == lax.broadcasted_iota(jnp.int32, (pblk, blk), 1)).astype(BF16)

    slope = srow_ref[:, 0:1]
    qq = lax.broadcasted_iota(jnp.int32, (rows, 1), 0) % ROWS_PER_HEAD
    q_pos = (past_len + qq).astype(F32)
    col = lax.broadcasted_iota(jnp.int32, (1, blk), 1).astype(F32)

    g = gate
    firsts, keeps = [], []
    for _ in range(MOBA_TOPK):
        m = jnp.max(g, axis=1, keepdims=True)
        first = jnp.min(jnp.where(g == m, lane_f, float(n_blocks)), axis=1, keepdims=True)
        hit = lane_f == first
        g = jnp.where(hit, -jnp.inf, g)
        firsts.append(first)
        keeps.append(jnp.dot(jnp.where(hit, 1.0, 0.0).astype(BF16), expand,
                             preferred_element_type=F32) > 0.5)

    lane_tiles = pblk // HEAD_DIM
    picked = [[jnp.zeros((rows, HEAD_DIM), F32)] * lane_tiles for _ in range(MOBA_TOPK)]
    for n in range(n_blocks):
        for v in range(lane_tiles):
            st_nv = st_ref[:, n * pblk + v * HEAD_DIM:n * pblk + (v + 1) * HEAD_DIM]
            for t in range(MOBA_TOPK):
                flag = keeps[t][:, n * HEAD_DIM:(n + 1) * HEAD_DIM]
                picked[t][v] = jnp.where(flag, st_nv, picked[t][v])

    scores = []
    for t in range(MOBA_TOPK):
        sc = jnp.concatenate(picked[t], axis=1)
        sc = _dot_onehot_exact(jnp.where(own_col, sc, 0.0), compact)
        dist = (q_pos - firsts[t] * float(blk)) - col
        scores.append(sc * ATTN_SCALE - slope * dist)

    j = lax.broadcasted_iota(jnp.int32, (rows, knew_ref.shape[0]), 1)
    so = lax.dot_general(qr.astype(BF16), knew_ref[...].astype(BF16), _NT,
                         preferred_element_type=F32) * ATTN_SCALE
    so = so - slope * (qq - j).astype(F32)
    so = jnp.where((j <= qq) & (j < n_new), so, NEG)

    m = jnp.max(so, axis=1, keepdims=True)
    for s in scores:
        m = jnp.maximum(m, jnp.max(s, axis=1, keepdims=True))
    po = jnp.exp(so - m)
    l = jnp.sum(po, axis=1, keepdims=True)
    for t, s in enumerate(scores):
        p = jnp.exp(s - m)
        l = l + jnp.sum(p, axis=1, keepdims=True)
        pc_ref[:, t * blk:(t + 1) * blk] = p
    pown_ref[...] = po
    l_ref[...] = jnp.broadcast_to(l, l_ref.shape)
    out_lane = lax.broadcasted_iota(jnp.int32, ids_ref.shape, 1)
    ids = jnp.zeros(ids_ref.shape, F32)
    for t, first in enumerate(firsts):
        ids = jnp.where(out_lane == t, first, ids)
    ids_ref[...] = ids.astype(jnp.int32)


def _select(qrows, ksum, st, knew, srow, n_new):
    db, rows, width = qrows.shape
    n_blocks = ksum.shape[1]
    past_len = st.shape[2]
    pad = knew.shape[1]
    assert n_blocks >= MOBA_TOPK and past_len == PAIRED * n_blocks * MOBA_BLOCK
    per_b = lambda *shape: pl.BlockSpec((None,) + shape, lambda b: (b,) + (0,) * len(shape))
    return pl.pallas_call(
        functools.partial(_select_kernel, n_new=n_new),
        grid=(db,),
        in_specs=[per_b(rows, width), per_b(n_blocks, width), per_b(rows, past_len),
                  per_b(pad, width), pl.BlockSpec((rows, HEAD_DIM), lambda b: (0, 0))],
        out_specs=[per_b(rows, MOBA_TOPK * MOBA_BLOCK), per_b(rows, pad), per_b(rows, HEAD_DIM),
                   per_b(rows, HEAD_DIM)],
        out_shape=[jax.ShapeDtypeStruct((db, rows, MOBA_TOPK * MOBA_BLOCK), F32),
                   jax.ShapeDtypeStruct((db, rows, pad), F32),
                   jax.ShapeDtypeStruct((db, rows, HEAD_DIM), F32),
                   jax.ShapeDtypeStruct((db, rows, HEAD_DIM), jnp.int32)],
        compiler_params=_params("arbitrary"),
        name="select",
    )(qrows, ksum, st, knew, srow)


def _gather_copy(picks_ref, pt_ref, cache_ref, v_buf, sem, sb, h, s, *, n_slots, n_pages, page_size):
    pages_per_block = MOBA_BLOCK // page_size
    block = picks_ref[(sb * N_HEADS + h) * (n_slots // pages_per_block) + s // pages_per_block]
    page = pt_ref[sb * n_pages + block * pages_per_block + s % pages_per_block]
    return pltpu.make_async_copy(cache_ref.at[page, :, h, :], v_buf.at[h, s], sem.at[h])


def _gather_attend(h, pc_ref, pown_ref, l_ref, vnew_ref, v_buf, o_ref, *, n_q, n_slots):
    page_size = v_buf.shape[2]
    heads_per_tile = SUBLANES // ROWS_PER_HEAD
    tile_rows = slice((h // heads_per_tile) * SUBLANES, (h // heads_per_tile + 1) * SUBLANES)
    base = (h % heads_per_tile) * ROWS_PER_HEAD
    cols = slice(h * HEAD_DIM, (h + 1) * HEAD_DIM)
    pc = pc_ref[tile_rows, :]
    row = lax.broadcasted_iota(jnp.int32, pc.shape, 0)
    p_all = jnp.concatenate([jnp.where(row == base + q, pc, 0.0) for q in range(n_q)], axis=1)
    v_all = v_buf[h].reshape(n_slots * page_size, HEAD_DIM)
    acc = (jnp.dot(pown_ref[tile_rows, :].astype(BF16), vnew_ref[:, cols].astype(BF16),
                   preferred_element_type=F32)
           + jnp.dot(p_all.astype(BF16), v_all.astype(BF16), preferred_element_type=F32))
    out = acc / l_ref[tile_rows, 0:1]
    o_ref[:, cols] = out[base:base + n_q, :]


def _sample_query_rows(q_s, db, n_q):
    assert n_q <= ROWS_PER_HEAD
    rows = ROWS_PER_HEAD * N_HEADS
    q4 = jnp.pad(q_s.reshape(db, n_q, N_HEADS, HEAD_DIM).transpose(0, 2, 1, 3),
                 ((0, 0), (0, 0), (0, ROWS_PER_HEAD - n_q), (0, 0)))
    place = lambda onehot: (q4[:, :, :, None, :] * onehot[None, :, None, :, None]
                            ).reshape(db, rows, onehot.shape[1] * HEAD_DIM)
    heads = jnp.arange(N_HEADS)
    own = (heads[:, None] == heads[None, :]).astype(F32)
    pair = (heads[:, None] % HEAD_PAIR == jnp.arange(HEAD_PAIR)[None, :]).astype(F32)
    return place(own), place(pair)


def _sample_probs(qrows, st, ksum, k_s, v_s, slopes, db, n_q):
    d_attn = N_HEADS * HEAD_DIM
    rows = ROWS_PER_HEAD * N_HEADS
    pad_new = ((0, 0), (0, HEAD_DIM - n_q), (0, 0))
    knew = jnp.pad(k_s.reshape(db, n_q, d_attn), pad_new)
    vnew = jnp.pad(v_s.reshape(db, n_q, d_attn), pad_new)
    srow = jnp.broadcast_to(jnp.repeat(slopes, ROWS_PER_HEAD)[:, None], (rows, HEAD_DIM))
    pc, pown, l, ids = _select(qrows, ksum, st, knew, srow, n_q)
    picks = ids[:, :, :MOBA_TOPK].reshape(db, N_HEADS, ROWS_PER_HEAD, MOBA_TOPK)[:, :, :n_q]
    return picks.reshape(-1), pc, pown, l, vnew


HALO_ROWS = 16


N_OUT_INPUTS = 19


def _out_kernel(*refs, sample, tiles_per_seq, n_q, gather):
    if gather is not None:
        picks_ref, pt_ref = refs[:2]
        refs = refs[2:]
        pc_ref, pown_ref, l_ref, vnew_ref, cache_ref = refs[N_OUT_INPUTS:N_OUT_INPUTS + 5]
        y_ref, cu_ref, attn_s_ref, v_buf, sem = refs[N_OUT_INPUTS + 5:]
        sb = pl.program_id(0)
        copy = functools.partial(_gather_copy, picks_ref, pt_ref, cache_ref, v_buf, sem,
                                 n_slots=gather["n_slots"], n_pages=gather["n_pages"],
                                 page_size=gather["page_size"])

        @pl.when(sb == 0)
        def _():
            for h in range(N_HEADS):
                for s in range(gather["n_slots"]):
                    copy(sb, h, s).start()
    else:
        y_ref, cu_ref = refs[N_OUT_INPUTS:]
    (attn_ref, za_ref, bg_ref, cg_ref, u_ref, zb_ref, ga0_ref, ga1_ref, gb0_ref, gb1_ref,
     h1_ref, h2_ref, x_ref, gate_ref, cw_ref, wpa_ref, wpb_ref, wo_ref, fg_ref) = refs[:N_OUT_INPUTS]
    tm = x_ref.shape[0]
    cu = cg_ref[...].astype(F32) * u_ref[...].astype(F32)
    row = lax.broadcasted_iota(jnp.int32, cu.shape, 0)
    prev1 = pltpu.roll(cu, 1, 0)
    prev2 = pltpu.roll(cu, 2, 0)
    if sample:
        t = row % n_q
        prev1 = jnp.where(t >= 1, prev1, h1_ref[...])
        prev2 = jnp.where(t >= 2, prev2, h2_ref[...])
        cu_ref[...] = cu
    else:
        first = (pl.program_id(0) % tiles_per_seq) == 0
        halo = h1_ref[...].astype(F32) * h2_ref[...].astype(F32)
        halo = jnp.where(first, 0.0, halo)
        before1 = halo[HALO_ROWS - 1:HALO_ROWS, :]
        before2 = halo[HALO_ROWS - 2:HALO_ROWS - 1, :]
        prev1 = jnp.where(row >= 1, prev1, before1)
        prev2 = jnp.where(row >= 2, prev2, jnp.where(row == 0, before2, before1))
        cu_ref[...] = cu[tm - (CONV_WIDTH - 1):, :]

    conv = bg_ref[...].astype(F32) * (cw_ref[0:1, :] * prev2 + cw_ref[1:2, :] * prev1
                                      + cw_ref[2:3, :] * cu)
    a_act = (attn_ref[...].astype(F32) * jax.nn.silu(za_ref[...].astype(F32))).astype(BF16)
    b_act = (conv * jax.nn.silu(zb_ref[...].astype(F32))).astype(BF16)
    ya = jnp.dot(a_act, wpa_ref[...], preferred_element_type=F32)
    yb = jnp.dot(b_act, wpb_ref[...], preferred_element_type=F32)
    ga = jnp.concatenate([ga0_ref[...], ga1_ref[...]], axis=1).astype(F32)
    gb = jnp.concatenate([gb0_ref[...], gb1_ref[...]], axis=1).astype(F32)
    merged = jax.nn.sigmoid(ga) * ya + jax.nn.sigmoid(gb) * yb
    out = jnp.dot(merged.astype(BF16), wo_ref[...], preferred_element_type=F32)
    xn = x_ref[...] + gate_ref[...] * out
    y = xn * lax.rsqrt(jnp.mean(xn * xn, axis=-1, keepdims=True) + NORM_EPS)
    y_ref[...] = y * fg_ref[...]

    if gather is not None:
        for h in range(N_HEADS):
            for s in range(gather["n_slots"]):
                copy(sb, h, s).wait()
            _gather_attend(h, pc_ref, pown_ref, l_ref, vnew_ref, v_buf, attn_s_ref,
                           n_q=gather["n_q"], n_slots=gather["n_slots"])

            @pl.when(sb + 1 < pl.num_programs(0))
            def _():
                for s in range(gather["n_slots"]):
                    copy(sb + 1, h, s).start()


def _out(attn, rest, h1, h2, x, gate, gate_spec, conv_w, w_pa, w_pb, w_o, final_g,
         *, sample, tm, tiles_per_seq=1, n_q=1, gather=None):
    m, d = x.shape
    dc = attn.shape[1]
    tiles = m // tm
    slab = lambda s: pl.BlockSpec((None, tm, dc), lambda i, *_: (s, i, 0))
    const = lambda shape: pl.BlockSpec(shape, lambda i, *_: (0,) * len(shape),
                                       pipeline_mode=pl.Buffered(1))
    if sample:
        hist_specs = [pl.BlockSpec((tm, dc), lambda i, *_: (i, 0))] * 2
        cu_spec = pl.BlockSpec((tm, dc), lambda i, *_: (i, 0))
        cu_shape = jax.ShapeDtypeStruct((m, dc), F32)
    else:
        per_tile = tm // HALO_ROWS
        halo = lambda s: pl.BlockSpec((None, HALO_ROWS, dc),
                                      lambda i, *_: (s, jnp.maximum(i * per_tile - 1, 0), 0))
        hist_specs = [halo(2), halo(3)]
        cu_spec = pl.BlockSpec((None, CONV_WIDTH - 1, dc), lambda i, *_: (i, 0, 0))
        cu_shape = jax.ShapeDtypeStruct((tiles, CONV_WIDTH - 1, dc), F32)
    in_specs = ([pl.BlockSpec((tm, dc), lambda i, *_: (i, 0))]
                + [slab(s) for s in (0, 1, 2, 3, 4, 5, 6, 7, 8)]
                + hist_specs
                + [pl.BlockSpec((tm, d), lambda i, *_: (i, 0)), gate_spec,
                   const(conv_w.shape), const(w_pa.shape), const(w_pb.shape), const(w_o.shape),
                   const(final_g.shape)])
    operands = [attn, *([rest] * 9), h1, h2, x, gate, conv_w, w_pa, w_pb, w_o, final_g]
    assert len(operands) == N_OUT_INPUTS
    out_specs = [pl.BlockSpec((tm, d), lambda i, *_: (i, 0)), cu_spec]
    out_shape = [jax.ShapeDtypeStruct((m, d), F32), cu_shape]
    scratch, prefetch, static = [], [], None
    if gather is not None:
        picks, pt, pc, pown, l, vnew, cache_v, gather_q, n_pages = gather
        db, rows, _ = pc.shape
        page_size = cache_v.shape[1]
        n_slots = gather_q * MOBA_TOPK * (MOBA_BLOCK // page_size)
        assert tiles == db, "one sample batch rides on each row tile"
        per_b = lambda a: pl.BlockSpec((None,) + a.shape[1:], lambda i, *_: (i, 0, 0))
        in_specs += [per_b(pc), per_b(pown), per_b(l), per_b(vnew), pl.BlockSpec(memory_space=pl.ANY)]
        operands += [pc, pown, l, vnew, cache_v]
        out_specs.append(pl.BlockSpec((None, gather_q, dc), lambda i, *_: (i, 0, 0)))
        out_shape.append(jax.ShapeDtypeStruct((db, gather_q, dc), F32))
        scratch = [pltpu.VMEM((N_HEADS, n_slots, page_size, HEAD_DIM), F32),
                   pltpu.SemaphoreType.DMA((N_HEADS,))]
        prefetch = [picks, pt]
        static = dict(n_slots=n_slots, n_pages=n_pages, page_size=page_size, n_q=gather_q)
    grid_spec = pltpu.PrefetchScalarGridSpec(
        num_scalar_prefetch=len(prefetch), grid=(tiles,), in_specs=in_specs, out_specs=out_specs,
        scratch_shapes=scratch)
    return pl.pallas_call(
        functools.partial(_out_kernel, sample=sample, tiles_per_seq=tiles_per_seq, n_q=n_q,
                          gather=static),
        grid_spec=grid_spec,
        out_shape=out_shape,
        compiler_params=_params("arbitrary"),
        name="out",
    )(*prefetch, *operands)


def kernel(x_prompt, x_sample, cache_k, cache_v, state_conv, page_table, c_prompt, c_sample,
           norm_g, w_ada, b_ada, w_in, conv_w, w_pa, w_pb, w_o, final_g):
    depth = norm_g.shape[0]
    assert depth == 1, "single-layer step only"
    batch, seq, d = x_prompt.shape
    db, n_q, _ = x_sample.shape
    n_pool, page_size, n_heads, head_dim = cache_k.shape[1:]
    assert (n_heads, head_dim) == (N_HEADS, HEAD_DIM)
    n_pages = page_table.shape[1]
    past_len = n_pages * page_size
    d_attn = n_heads * head_dim
    d_conv = d - d_attn
    assert d_attn == d_conv == 1024 and w_in.shape[2] == 12 * 1024
    assert seq % MOBA_BLOCK == 0 and past_len % MOBA_BLOCK == 0 and n_q <= MOBA_BLOCK

    slopes = jnp.exp2(-8.0 * jnp.arange(1, N_HEADS + 1, dtype=F32) / N_HEADS)

    n_c = batch + db
    c_all = jnp.pad(jnp.concatenate([c_prompt, c_sample], axis=0), ((0, -n_c % 8), (0, 0)))
    mod = _ada(c_all, w_ada[0], b_ada[0][None, :])
    mod4 = mod.reshape(mod.shape[0], 3, 1, d)
    mod_tok = jnp.repeat(mod[batch:n_c].reshape(db, 3, d), n_q, axis=0)

    xp = x_prompt.reshape(batch * seq, d)
    xs = x_sample.reshape(db * n_q, d)
    g = norm_g[0][None, :]
    tm_prep = 512
    per_seq = seq // tm_prep
    mod_spec = lambda s: pl.BlockSpec((None, None, 1, d), lambda i: (i // per_seq, s, 0, 0))
    h_p = _prep(xp, g, mod4, mod4, mod_spec(1), mod_spec(0), tm_prep)
    tok_spec = pl.BlockSpec((db * n_q, d), lambda i: (0, 0))
    h_s = _prep(xs, g, mod_tok[:, 1], mod_tok[:, 0], tok_spec, tok_spec, db * n_q)

    w = w_in[0]
    (q_p,), (q_s,) = _proj(h_p, h_s, w, 0, 1, F32)
    (k_p,), (k_s,) = _proj(h_p, h_s, w, 1, 1, F32)
    (v_p,), (v_s,) = _proj(h_p, h_s, w, 2, 1, F32)
    rest_p, rest_s = _proj(h_p, h_s, w, 3, 9, BF16)

    ck = cache_k[0].reshape(n_pool, page_size * N_HEADS, HEAD_DIM)
    pt = page_table.reshape(-1).astype(jnp.int32)
    qrows, qpair = _sample_query_rows(q_s, db, n_q)
    attn_p, st, ksum = _attn_p(q_p, k_p, v_p, slopes, batch, seq, pt, qpair, ck, n_pages)
    picks, pc, pown, l, vnew = _sample_probs(qrows, st, ksum, k_s, v_s, slopes, db, n_q)

    cw = conv_w[0]
    wpa, wpb, wo = w_pa[0].astype(BF16), w_pb[0].astype(BF16), w_o[0].astype(BF16)
    fg = final_g[None, :]
    tm_out = batch * seq // db
    assert seq % tm_out == 0 and tm_out % HALO_ROWS == 0
    tiles_per_seq = seq // tm_out
    gate_p_spec = pl.BlockSpec((None, None, 1, d), lambda i, *_: (i // tiles_per_seq, 2, 0, 0))
    y_p, cu_tail, attn_s = _out(
        attn_p, rest_p, rest_p, rest_p, xp, mod4, gate_p_spec, cw, wpa, wpb, wo, fg,
        sample=False, tm=tm_out, tiles_per_seq=tiles_per_seq,
        gather=(picks, pt, pc, pown, l, vnew, cache_v[0], n_q, n_pages))
    attn_s = attn_s.reshape(db * n_q, d_attn)

    state = state_conv[0]
    zeros = jnp.zeros((db, n_q - 1, d_conv), F32)
    hist1 = jnp.concatenate([state[:, 1:2], zeros], axis=1).reshape(db * n_q, d_conv)
    hist2 = jnp.concatenate([state, zeros[:, 1:]], axis=1).reshape(db * n_q, d_conv)
    gate_s_spec = pl.BlockSpec((db * n_q, d), lambda i, *_: (0, 0))
    y_s, cu_s = _out(attn_s, rest_s, hist1, hist2, xs, mod_tok[:, 2], gate_s_spec, cw, wpa, wpb, wo, fg,
                     sample=True, tm=db * n_q, n_q=n_q)

    conv_p = cu_tail.reshape(batch, tiles_per_seq, CONV_WIDTH - 1, d_conv)[:, -1]
    conv_s = cu_s.reshape(db, n_q, d_conv)[:, n_q - (CONV_WIDTH - 1):]
    kv_p = (1, batch, seq, N_HEADS, HEAD_DIM)
    kv_s = (1, db, n_q, N_HEADS, HEAD_DIM)
    return (y_p.reshape(batch, seq, d), y_s.reshape(db, n_q, d),
            k_p.reshape(kv_p), v_p.reshape(kv_p), conv_p[None],
            k_s.reshape(kv_s), v_s.reshape(kv_s), conv_s[None])
```

```python
import functools

import jax
import jax.numpy as jnp
from jax import lax
from jax.experimental import pallas as pl
from jax.experimental.pallas import tpu as pltpu

F32 = jnp.float32
BF16 = jnp.bfloat16

N_HEADS = 8
HEAD_DIM = 128
MOBA_BLOCK = 256
MOBA_TOPK = 3
CONV_WIDTH = 3
NORM_EPS = 1e-6
NEG = float(jnp.finfo(jnp.float32).min)
MASKED = -1e30
ATTN_SCALE = HEAD_DIM ** -0.5
LOG2_E = 1.4426950408889634
SCALE2 = ATTN_SCALE * LOG2_E
Q_TILES = 2

VMEM_LIMIT_BYTES = 56 * 1024 * 1024

_NT = (((1,), (1,)), ((), ()))


def _params(*semantics):
    return pltpu.CompilerParams(dimension_semantics=semantics, vmem_limit_bytes=VMEM_LIMIT_BYTES)


def _ada_kernel(c_ref, w_ref, b_ref, o_ref):
    o_ref[...] = jnp.dot(c_ref[...].astype(BF16), w_ref[...].astype(BF16),
                         preferred_element_type=F32) + b_ref[...]


def _ada(c_all, w_ada, b_ada):
    rows, d = c_all.shape
    n = w_ada.shape[1]
    tn = 1024
    return pl.pallas_call(
        _ada_kernel,
        grid=(n // tn,),
        in_specs=[pl.BlockSpec((rows, d), lambda j: (0, 0)),
                  pl.BlockSpec((d, tn), lambda j: (0, j)),
                  pl.BlockSpec((1, tn), lambda j: (0, j))],
        out_specs=pl.BlockSpec((rows, tn), lambda j: (0, j)),
        out_shape=jax.ShapeDtypeStruct((rows, n), F32),
        compiler_params=_params("arbitrary"),
        name="ada",
    )(c_all, w_ada, b_ada)


def _prep_kernel(x_ref, g_ref, scale_ref, shift_ref, h_ref):
    x = x_ref[...]
    y = x * lax.rsqrt(jnp.mean(x * x, axis=-1, keepdims=True) + NORM_EPS)
    h = (y * g_ref[...]) * (1.0 + scale_ref[...]) + shift_ref[...]
    h_ref[...] = h.astype(BF16)


def _prep(x, g, scale, shift, scale_spec, shift_spec, tm):
    m, d = x.shape
    return pl.pallas_call(
        _prep_kernel,
        grid=(m // tm,),
        in_specs=[pl.BlockSpec((tm, d), lambda i: (i, 0)),
                  pl.BlockSpec((1, d), lambda i: (0, 0)),
                  scale_spec, shift_spec],
        out_specs=pl.BlockSpec((tm, d), lambda i: (i, 0)),
        out_shape=jax.ShapeDtypeStruct((m, d), BF16),
        compiler_params=_params("arbitrary"),
        name="prep",
    )(x, g, scale, shift)


def _proj_kernel(hp_ref, hs_ref, w_ref, op_ref, os_ref, wb_ref):
    @pl.when(pl.program_id(1) == 0)
    def _():
        wb_ref[...] = w_ref[...].astype(BF16)
        os_ref[...] = jnp.dot(hs_ref[...], wb_ref[...], preferred_element_type=F32).astype(os_ref.dtype)

    op_ref[...] = jnp.dot(hp_ref[...], wb_ref[...], preferred_element_type=F32).astype(op_ref.dtype)


def _proj(h_p, h_s, w, col0, ncol, dtype, tm=1024, tn=1024):
    mp, d = h_p.shape
    ms = h_s.shape[0]
    return pl.pallas_call(
        _proj_kernel,
        grid=(ncol, mp // tm),
        in_specs=[pl.BlockSpec((tm, d), lambda j, i: (i, 0)),
                  pl.BlockSpec((ms, d), lambda j, i: (0, 0)),
                  pl.BlockSpec((d, tn), lambda j, i: (0, col0 + j))],
        out_specs=[pl.BlockSpec((None, tm, tn), lambda j, i: (j, i, 0)),
                   pl.BlockSpec((None, ms, tn), lambda j, i: (j, 0, 0))],
        out_shape=[jax.ShapeDtypeStruct((ncol, mp, tn), dtype),
                   jax.ShapeDtypeStruct((ncol, ms, tn), dtype)],
        scratch_shapes=[pltpu.VMEM((d, tn), BF16)],
        compiler_params=_params("arbitrary", "arbitrary"),
        name="proj",
    )(h_p, h_s, w)


def _topk_mask(gate, valid, idx_f, n, axis):
    g = jnp.where(valid, gate, -jnp.inf)
    sel = jnp.zeros(gate.shape, F32)
    for _ in range(MOBA_TOPK):
        m = jnp.max(g, axis=axis, keepdims=True)
        first = jnp.min(jnp.where(g == m, idx_f, float(n)), axis=axis, keepdims=True)
        hit = idx_f == first
        sel = jnp.where(hit & valid, 1.0, sel)
        g = jnp.where(hit, -jnp.inf, g)
    return sel


SUBLANES = 8
ROWS_PER_HEAD = 4
HEAD_PAIR = N_HEADS // 2
PAIRED = N_HEADS // HEAD_PAIR


def _kstream_repack(k_refs, ksum_ref, kb_ref, page_size):
    pages_per_block = MOBA_BLOCK // page_size
    pair_rows = PAIRED * page_size
    row_hh = lax.broadcasted_iota(jnp.int32, (SUBLANES, HEAD_DIM), 0) % PAIRED
    for blk in range(len(k_refs) // pages_per_block):
        for g in range(HEAD_PAIR):
            tot = None
            for j in range(blk * pages_per_block, (blk + 1) * pages_per_block):
                x = k_refs[j][pl.ds(g, pair_rows, stride=HEAD_PAIR), :]
                kb_ref[j * pair_rows:(j + 1) * pair_rows, g * HEAD_DIM:(g + 1) * HEAD_DIM] = (
                    x.astype(BF16))
                part = jnp.sum(x.reshape(pair_rows // SUBLANES, SUBLANES, HEAD_DIM), axis=0)
                tot = part if tot is None else tot + part
            for hh in range(PAIRED):
                head = g + HEAD_PAIR * hh
                ksum_ref[blk:blk + 1, head * HEAD_DIM:(head + 1) * HEAD_DIM] = jnp.sum(
                    jnp.where(row_hh == hh, tot, 0.0), axis=0, keepdims=True)


def _kstream_scores(qpair_ref, kb_ref, st_ref):
    st_ref[...] = lax.dot_general(qpair_ref[...].astype(BF16), kb_ref[...], _NT,
                                  preferred_element_type=F32)


def _attn_p_kernel(pt_ref, slopes_ref, q_ref, k_ref, v_ref, qrows_ref, *refs,
                   n_blocks, n_stream, page_size):
    kp_refs = refs[:n_stream]
    o_ref, st_ref, ksum_ref, ka_ref, vb_ref, kmean_ref, eye_ref, qa_ref, s_ref, kb_ref = refs[n_stream:]
    blk = MOBA_BLOCK
    half = blk // 2
    h = pl.program_id(1)
    group = pl.program_id(2)

    @pl.when(group == 0)
    def _():
        lane = lax.broadcasted_iota(jnp.int32, (blk, HEAD_DIM), 1)
        for n in range(n_blocks):
            rows = slice(n * blk, (n + 1) * blk)
            kf = k_ref[rows, :]
            ka_ref[rows, :HEAD_DIM] = kf.astype(BF16)
            ka_ref[rows, HEAD_DIM:] = jnp.where(lane == n, 1.0, 0.0).astype(BF16)
            vb_ref[rows, :] = v_ref[rows, :].astype(BF16)
            kmean_ref[n:n + 1, :] = jnp.sum(kf, axis=0, keepdims=True) * (1.0 / blk)
        eye_ref[...] = jnp.where(lax.broadcasted_iota(jnp.int32, (blk, blk), 0)
                                 == lax.broadcasted_iota(jnp.int32, (blk, blk), 1),
                                 1.0, 0.0).astype(BF16)

    _kstream_repack(kp_refs, ksum_ref, kb_ref, page_size)

    slope2 = slopes_ref[h] * LOG2_E
    for t in range(Q_TILES):
        qi = group * Q_TILES + t
        qf = q_ref[t * blk:(t + 1) * blk, :]
        gate_t = lax.dot_general(kmean_ref[...], qf, _NT, precision=lax.Precision.HIGHEST,
                                 preferred_element_type=F32)
        row = lax.broadcasted_iota(jnp.int32, gate_t.shape, 0)
        sel_t = _topk_mask(gate_t, row < qi, row.astype(F32), n_blocks, axis=0)
        unpicked_t = jnp.concatenate([jnp.where(sel_t > 0.5, 0.0, MASKED),
                                      jnp.zeros((HEAD_DIM - n_blocks, blk), F32)], axis=0)
        unpicked = lax.dot_general(eye_ref[...], unpicked_t.astype(BF16), _NT,
                                   preferred_element_type=F32)
        qa_ref[t, :, :HEAD_DIM] = qf.astype(BF16)
        qa_ref[t, :, HEAD_DIM:] = unpicked.astype(BF16)
    _kstream_scores(qrows_ref, kb_ref, st_ref)

    def halves(x):
        return x[:, :half], x[:, half:]

    def attend(first_qi):
        key_col = lax.broadcasted_iota(jnp.int32, (1, blk), 1).astype(F32)
        col_bias = slope2 * key_col
        rel = (lax.broadcasted_iota(jnp.int32, (blk, blk), 0)
               - lax.broadcasted_iota(jnp.int32, (blk, blk), 1))
        tiles = [(t, first_qi + t) for t in range(Q_TILES)]

        own, m = {}, {}
        for t, qi in tiles:
            s_own = lax.dot_general(qa_ref[t, :, :HEAD_DIM], ka_ref[qi * blk:(qi + 1) * blk, :HEAD_DIM],
                                    _NT, preferred_element_type=F32) * SCALE2 + col_bias
            own[t] = jnp.where(rel >= 0, s_own, NEG)
            lo, hi = halves(own[t])
            mx = jnp.maximum(lo, hi)
            for n in range(qi):
                bias_n = col_bias + slope2 * float((n - qi) * blk)
                s = lax.dot_general(qa_ref[t], ka_ref[n * blk:(n + 1) * blk, :], _NT,
                                    preferred_element_type=F32) * SCALE2 + bias_n
                s_ref[t, n] = s
                lo, hi = halves(s)
                mx = jnp.maximum(mx, jnp.maximum(lo, hi))
            m[t] = jnp.broadcast_to(jnp.max(mx, axis=1, keepdims=True), (blk, half))

        def probs(s, m_t):
            lo, hi = halves(s)
            lo, hi = jnp.exp2(lo - m_t), jnp.exp2(hi - m_t)
            return lo + hi, jnp.concatenate([lo, hi], axis=1).astype(BF16)

        for t, qi in tiles:
            lsum, p = probs(own[t], m[t])
            acc = jnp.dot(p, vb_ref[qi * blk:(qi + 1) * blk, :], preferred_element_type=F32)
            for n in range(qi):
                part, p = probs(s_ref[t, n], m[t])
                lsum = lsum + part
                acc = acc + jnp.dot(p, vb_ref[n * blk:(n + 1) * blk, :], preferred_element_type=F32)
            l = jnp.sum(lsum, axis=1, keepdims=True)
            o_ref[t * blk:(t + 1) * blk, :] = (acc / l).astype(o_ref.dtype)

    for g in range(n_blocks // Q_TILES):
        pl.when(group == g)(functools.partial(attend, g * Q_TILES))


def _attn_p(qkv, slopes, batch, seq, pt, qpair, cache_k, n_pages):
    n_blocks = seq // MOBA_BLOCK
    blk = MOBA_BLOCK
    assert n_blocks % Q_TILES == 0 and n_blocks <= HEAD_DIM
    n_groups = n_blocks // Q_TILES
    db, rows, pair_width = qpair.shape
    width = N_HEADS * HEAD_DIM
    page_size = cache_k.shape[1] // N_HEADS
    n_steps = batch * N_HEADS * n_groups
    n_stream = db * n_pages // n_steps
    assert n_stream * n_steps == db * n_pages and n_pages % n_stream == 0
    assert (n_stream * page_size) % MOBA_BLOCK == 0
    steps_per_sb = n_pages // n_stream
    keys_per_step = n_stream * page_size
    blocks_per_step = keys_per_step // MOBA_BLOCK

    def stream_pos(b, h, qi):
        lin = (b * N_HEADS + h) * n_groups + qi
        return lin // steps_per_sb, lin % steps_per_sb

    def page_spec(j):
        def index(b, h, qi, pt):
            sb, c = stream_pos(b, h, qi)
            return pt[sb * n_pages + c * n_stream + j], 0, 0
        return pl.BlockSpec((None, page_size * N_HEADS, HEAD_DIM), index)

    kv_spec = lambda which: pl.BlockSpec((None, seq, HEAD_DIM), lambda b, h, qi, pt: (which, b, h))
    q_spec = pl.BlockSpec((None, Q_TILES * blk, HEAD_DIM),
                          lambda b, h, qi, pt: (0, b * n_groups + qi, h))
    o_spec = pl.BlockSpec((Q_TILES * blk, HEAD_DIM), lambda b, h, qi, pt: (b * n_groups + qi, h))
    grid_spec = pltpu.PrefetchScalarGridSpec(
        num_scalar_prefetch=1,
        grid=(batch, N_HEADS, n_groups),
        in_specs=[pl.BlockSpec(memory_space=pltpu.SMEM), q_spec, kv_spec(1), kv_spec(2),
                  pl.BlockSpec((None, rows, pair_width),
                               lambda b, h, qi, pt: (stream_pos(b, h, qi)[0], 0, 0))]
        + [page_spec(j) for j in range(n_stream)],
        out_specs=[o_spec,
                   pl.BlockSpec((None, rows, PAIRED * keys_per_step),
                                lambda b, h, qi, pt: (stream_pos(b, h, qi)[0], 0, stream_pos(b, h, qi)[1])),
                   pl.BlockSpec((None, None, blocks_per_step, width),
                                lambda b, h, qi, pt: stream_pos(b, h, qi) + (0, 0))],
        scratch_shapes=[pltpu.VMEM((seq, 2 * HEAD_DIM), BF16),
                        pltpu.VMEM((seq, HEAD_DIM), BF16),
                        pltpu.VMEM((n_blocks, HEAD_DIM), F32),
                        pltpu.VMEM((blk, blk), BF16),
                        pltpu.VMEM((Q_TILES, blk, 2 * HEAD_DIM), BF16),
                        pltpu.VMEM((Q_TILES, n_blocks, blk, blk), F32),
                        pltpu.VMEM((PAIRED * keys_per_step, pair_width), BF16)],
    )
    attn, st, ksum = pl.pallas_call(
        functools.partial(_attn_p_kernel, n_blocks=n_blocks, n_stream=n_stream, page_size=page_size),
        grid_spec=grid_spec,
        out_shape=[jax.ShapeDtypeStruct(qkv.shape[1:], BF16),
                   jax.ShapeDtypeStruct((db, rows, PAIRED * n_pages * page_size), F32),
                   jax.ShapeDtypeStruct((db, steps_per_sb, blocks_per_step, width), F32)],
        compiler_params=_params("arbitrary", "arbitrary", "arbitrary"),
        name="attn_p",
    )(pt, slopes, qkv, qkv, qkv, qpair, *([cache_k] * n_stream))
    return attn, st, ksum.reshape(db, steps_per_sb * blocks_per_step, width)


def _dot_onehot_exact(x, onehot):
    hi = x.astype(BF16)
    rest = x - hi.astype(F32)
    mid = rest.astype(BF16)
    lo = (rest - mid.astype(F32)).astype(BF16)
    return ((jnp.dot(hi, onehot, preferred_element_type=F32)
             + jnp.dot(mid, onehot, preferred_element_type=F32))
            + jnp.dot(lo, onehot, preferred_element_type=F32))


def _select_kernel(qrows_ref, ksum_ref, st_ref, knew_ref, srow_ref,
                   pc_ref, pown_ref, l_ref, ids_ref, *, n_new):
    rows = st_ref.shape[0]
    n_blocks = ksum_ref.shape[0]
    blk = MOBA_BLOCK
    past_len = n_blocks * blk
    pblk = PAIRED * blk
    qr = qrows_ref[...]
    kmean = ksum_ref[...] * (1.0 / blk)
    gate = lax.dot_general(qr, kmean, _NT, precision=lax.Precision.HIGHEST,
                           preferred_element_type=F32)
    lane_f = lax.broadcasted_iota(jnp.int32, gate.shape, 1).astype(F32)
    expand = (lax.broadcasted_iota(jnp.int32, (n_blocks, n_blocks * HEAD_DIM), 1) // HEAD_DIM
              == lax.broadcasted_iota(jnp.int32, (n_blocks, n_blocks * HEAD_DIM), 0)).astype(BF16)
    own_col = (lax.broadcasted_iota(jnp.int32, (rows, pblk), 1) % PAIRED
               == lax.broadcasted_iota(jnp.int32, (rows, pblk), 0) // (ROWS_PER_HEAD * HEAD_PAIR))
    compact = (lax.broadcasted_iota(jnp.int32, (pblk, blk), 0) // PAIRED
               == lax.broadcasted_iota(jnp.int32, (pblk, blk), 1)).astype(BF16)

    slope = srow_ref[:, 0:1]
    qq = lax.broadcasted_iota(jnp.int32, (rows, 1), 0) % ROWS_PER_HEAD
    q_pos = (past_len + qq).astype(F32)
    col = lax.broadcasted_iota(jnp.int32, (1, blk), 1).astype(F32)

    g = gate
    firsts, keeps = [], []
    for _ in range(MOBA_TOPK):
        m = jnp.max(g, axis=1, keepdims=True)
        first = jnp.min(jnp.where(g == m, lane_f, float(n_blocks)), axis=1, keepdims=True)
        hit = lane_f == first
        g = jnp.where(hit, -jnp.inf, g)
        firsts.append(first)
        keeps.append(jnp.dot(jnp.where(hit, 1.0, 0.0).astype(BF16), expand,
                             preferred_element_type=F32) > 0.5)

    lane_tiles = pblk // HEAD_DIM
    picked = [[jnp.zeros((rows, HEAD_DIM), F32)] * lane_tiles for _ in range(MOBA_TOPK)]
    for n in range(n_blocks):
        for v in range(lane_tiles):
            st_nv = st_ref[:, n * pblk + v * HEAD_DIM:n * pblk + (v + 1) * HEAD_DIM]
            for t in range(MOBA_TOPK):
                flag = keeps[t][:, n * HEAD_DIM:(n + 1) * HEAD_DIM]
                picked[t][v] = jnp.where(flag, st_nv, picked[t][v])

    scores = []
    for t in range(MOBA_TOPK):
        sc = jnp.concatenate(picked[t], axis=1)
        sc = _dot_onehot_exact(jnp.where(own_col, sc, 0.0), compact)
        dist = (q_pos - firsts[t] * float(blk)) - col
        scores.append(sc * ATTN_SCALE - slope * dist)

    j = lax.broadcasted_iota(jnp.int32, (rows, knew_ref.shape[0]), 1)
    so = lax.dot_general(qr.astype(BF16), knew_ref[...].astype(BF16), _NT,
                         preferred_element_type=F32) * ATTN_SCALE
    so = so - slope * (qq - j).astype(F32)
    so = jnp.where((j <= qq) & (j < n_new), so, NEG)

    m = jnp.max(so, axis=1, keepdims=True)
    for s in scores:
        m = jnp.maximum(m, jnp.max(s, axis=1, keepdims=True))
    po = jnp.exp(so - m)
    l = jnp.sum(po, axis=1, keepdims=True)
    for t, s in enumerate(scores):
        p = jnp.exp(s - m)
        l = l + jnp.sum(p, axis=1, keepdims=True)
        pc_ref[:, t * blk:(t + 1) * blk] = p
    pown_ref[...] = po
    l_ref[...] = jnp.broadcast_to(l, l_ref.shape)
    out_lane = lax.broadcasted_iota(jnp.int32, ids_ref.shape, 1)
    ids = jnp.zeros(ids_ref.shape, F32)
    for t, first in enumerate(firsts):
        ids = jnp.where(out_lane == t, first, ids)
    ids_ref[...] = ids.astype(jnp.int32)


def _select(qrows, ksum, st, knew, srow, n_new):
    db, rows, width = qrows.shape
    n_blocks = ksum.shape[1]
    past_len = st.shape[2]
    pad = knew.shape[1]
    assert n_blocks >= MOBA_TOPK and past_len == PAIRED * n_blocks * MOBA_BLOCK
    per_b = lambda *shape: pl.BlockSpec((None,) + shape, lambda b: (b,) + (0,) * len(shape))
    return pl.pallas_call(
        functools.partial(_select_kernel, n_new=n_new),
        grid=(db,),
        in_specs=[per_b(rows, width), per_b(n_blocks, width), per_b(rows, past_len),
                  per_b(pad, width), pl.BlockSpec((rows, HEAD_DIM), lambda b: (0, 0))],
        out_specs=[per_b(rows, MOBA_TOPK * MOBA_BLOCK), per_b(rows, pad), per_b(rows, HEAD_DIM),
                   per_b(rows, HEAD_DIM)],
        out_shape=[jax.ShapeDtypeStruct((db, rows, MOBA_TOPK * MOBA_BLOCK), F32),
                   jax.ShapeDtypeStruct((db, rows, pad), F32),
                   jax.ShapeDtypeStruct((db, rows, HEAD_DIM), F32),
                   jax.ShapeDtypeStruct((db, rows, HEAD_DIM), jnp.int32)],
        compiler_params=_params("arbitrary"),
        name="select",
    )(qrows, ksum, st, knew, srow)


def _gather_copy(picks_ref, pt_ref, cache_ref, v_buf, sem, sb, h, s, *, n_slots, n_pages, page_size):
    pages_per_block = MOBA_BLOCK // page_size
    block = picks_ref[(sb * N_HEADS + h) * (n_slots // pages_per_block) + s // pages_per_block]
    page = pt_ref[sb * n_pages + block * pages_per_block + s % pages_per_block]
    return pltpu.make_async_copy(cache_ref.at[page, :, h, :], v_buf.at[h, s], sem.at[h])


def _gather_attend(h, pc_ref, pown_ref, l_ref, vnew_ref, v_buf, o_ref, *, n_q, n_slots):
    page_size = v_buf.shape[2]
    heads_per_tile = SUBLANES // ROWS_PER_HEAD
    tile_rows = slice((h // heads_per_tile) * SUBLANES, (h // heads_per_tile + 1) * SUBLANES)
    base = (h % heads_per_tile) * ROWS_PER_HEAD
    cols = slice(h * HEAD_DIM, (h + 1) * HEAD_DIM)
    pc = pc_ref[tile_rows, :]
    row = lax.broadcasted_iota(jnp.int32, pc.shape, 0)
    p_all = jnp.concatenate([jnp.where(row == base + q, pc, 0.0) for q in range(n_q)], axis=1)
    v_all = v_buf[h].reshape(n_slots * page_size, HEAD_DIM)
    acc = (jnp.dot(pown_ref[tile_rows, :].astype(BF16), vnew_ref[:, cols].astype(BF16),
                   preferred_element_type=F32)
           + jnp.dot(p_all.astype(BF16), v_all.astype(BF16), preferred_element_type=F32))
    out = acc / l_ref[tile_rows, 0:1]
    o_ref[:, cols] = out[base:base + n_q, :]


def _sample_query_rows(q_s, db, n_q):
    assert n_q <= ROWS_PER_HEAD
    rows = ROWS_PER_HEAD * N_HEADS
    q4 = jnp.pad(q_s.reshape(db, n_q, N_HEADS, HEAD_DIM).transpose(0, 2, 1, 3),
                 ((0, 0), (0, 0), (0, ROWS_PER_HEAD - n_q), (0, 0)))
    place = lambda onehot: (q4[:, :, :, None, :] * onehot[None, :, None, :, None]
                            ).reshape(db, rows, onehot.shape[1] * HEAD_DIM)
    heads = jnp.arange(N_HEADS)
    own = (heads[:, None] == heads[None, :]).astype(F32)
    pair = (heads[:, None] % HEAD_PAIR == jnp.arange(HEAD_PAIR)[None, :]).astype(F32)
    return place(own), place(pair)


def _sample_probs(qrows, st, ksum, k_s, v_s, slopes, db, n_q):
    d_attn = N_HEADS * HEAD_DIM
    rows = ROWS_PER_HEAD * N_HEADS
    pad_new = ((0, 0), (0, -n_q % SUBLANES), (0, 0))
    knew = jnp.pad(k_s.reshape(db, n_q, d_attn), pad_new)
    vnew = jnp.pad(v_s.reshape(db, n_q, d_attn), pad_new)
    srow = jnp.broadcast_to(jnp.repeat(slopes, ROWS_PER_HEAD)[:, None], (rows, HEAD_DIM))
    pc, pown, l, ids = _select(qrows, ksum, st, knew, srow, n_q)
    picks = ids[:, :, :MOBA_TOPK].reshape(db, N_HEADS, ROWS_PER_HEAD, MOBA_TOPK)[:, :, :n_q]
    return picks.reshape(-1), pc, pown, l, vnew


HALO_ROWS = 16


N_OUT_INPUTS = 19


def _out_kernel(*refs, sample, tiles_per_seq, n_q, gather):
    if gather is not None:
        picks_ref, pt_ref = refs[:2]
        refs = refs[2:]
        pc_ref, pown_ref, l_ref, vnew_ref, cache_ref = refs[N_OUT_INPUTS:N_OUT_INPUTS + 5]
        y_ref, cu_ref, attn_s_ref, v_buf, sem = refs[N_OUT_INPUTS + 5:]
        sb = pl.program_id(0)
        copy = functools.partial(_gather_copy, picks_ref, pt_ref, cache_ref, v_buf, sem,
                                 n_slots=gather["n_slots"], n_pages=gather["n_pages"],
                                 page_size=gather["page_size"])

        @pl.when(sb == 0)
        def _():
            for h in range(N_HEADS):
                for s in range(gather["n_slots"]):
                    copy(sb, h, s).start()
    else:
        y_ref, cu_ref = refs[N_OUT_INPUTS:]
    (attn_ref, za_ref, bg_ref, cg_ref, u_ref, zb_ref, ga0_ref, ga1_ref, gb0_ref, gb1_ref,
     h1_ref, h2_ref, x_ref, gate_ref, cw_ref, wpa_ref, wpb_ref, wo_ref, fg_ref) = refs[:N_OUT_INPUTS]
    tm = x_ref.shape[0]
    cu = cg_ref[...].astype(F32) * u_ref[...].astype(F32)
    row = lax.broadcasted_iota(jnp.int32, cu.shape, 0)
    prev1 = pltpu.roll(cu, 1, 0)
    prev2 = pltpu.roll(cu, 2, 0)
    if sample:
        t = row % n_q
        prev1 = jnp.where(t >= 1, prev1, h1_ref[...])
        prev2 = jnp.where(t >= 2, prev2, h2_ref[...])
        cu_ref[...] = cu
    else:
        first = (pl.program_id(0) % tiles_per_seq) == 0
        halo = h1_ref[...].astype(F32) * h2_ref[...].astype(F32)
        halo = jnp.where(first, 0.0, halo)
        before1 = halo[HALO_ROWS - 1:HALO_ROWS, :]
        before2 = halo[HALO_ROWS - 2:HALO_ROWS - 1, :]
        prev1 = jnp.where(row >= 1, prev1, before1)
        prev2 = jnp.where(row >= 2, prev2, jnp.where(row == 0, before2, before1))
        cu_ref[...] = cu[tm - (CONV_WIDTH - 1):, :]

    conv = bg_ref[...].astype(F32) * (cw_ref[0:1, :] * prev2 + cw_ref[1:2, :] * prev1
                                      + cw_ref[2:3, :] * cu)
    a_act = (attn_ref[...].astype(F32) * jax.nn.silu(za_ref[...].astype(F32))).astype(BF16)
    b_act = (conv * jax.nn.silu(zb_ref[...].astype(F32))).astype(BF16)
    ya = jnp.dot(a_act, wpa_ref[...], preferred_element_type=F32)
    yb = jnp.dot(b_act, wpb_ref[...], preferred_element_type=F32)
    ga = jnp.concatenate([ga0_ref[...], ga1_ref[...]], axis=1).astype(F32)
    gb = jnp.concatenate([gb0_ref[...], gb1_ref[...]], axis=1).astype(F32)
    merged = jax.nn.sigmoid(ga) * ya + jax.nn.sigmoid(gb) * yb
    out = jnp.dot(merged.astype(BF16), wo_ref[...], preferred_element_type=F32)
    xn = x_ref[...] + gate_ref[...] * out
    y = xn * lax.rsqrt(jnp.mean(xn * xn, axis=-1, keepdims=True) + NORM_EPS)
    y_ref[...] = y * fg_ref[...]

    if gather is not None:
        for h in range(N_HEADS):
            for s in range(gather["n_slots"]):
                copy(sb, h, s).wait()
            _gather_attend(h, pc_ref, pown_ref, l_ref, vnew_ref, v_buf, attn_s_ref,
                           n_q=gather["n_q"], n_slots=gather["n_slots"])

            @pl.when(sb + 1 < pl.num_programs(0))
            def _():
                for s in range(gather["n_slots"]):
                    copy(sb + 1, h, s).start()


def _out(attn, rest, h1, h2, x, gate, gate_spec, conv_w, w_pa, w_pb, w_o, final_g,
         *, sample, tm, tiles_per_seq=1, n_q=1, gather=None):
    m, d = x.shape
    dc = attn.shape[1]
    tiles = m // tm
    slab = lambda s: pl.BlockSpec((None, tm, dc), lambda i, *_: (s, i, 0))
    const = lambda shape: pl.BlockSpec(shape, lambda i, *_: (0,) * len(shape),
                                       pipeline_mode=pl.Buffered(1))
    if sample:
        hist_specs = [pl.BlockSpec((tm, dc), lambda i, *_: (i, 0))] * 2
        cu_spec = pl.BlockSpec((tm, dc), lambda i, *_: (i, 0))
        cu_shape = jax.ShapeDtypeStruct((m, dc), F32)
    else:
        per_tile = tm // HALO_ROWS
        halo = lambda s: pl.BlockSpec((None, HALO_ROWS, dc),
                                      lambda i, *_: (s, jnp.maximum(i * per_tile - 1, 0), 0))
        hist_specs = [halo(2), halo(3)]
        cu_spec = pl.BlockSpec((None, CONV_WIDTH - 1, dc), lambda i, *_: (i, 0, 0))
        cu_shape = jax.ShapeDtypeStruct((tiles, CONV_WIDTH - 1, dc), F32)
    in_specs = ([pl.BlockSpec((tm, dc), lambda i, *_: (i, 0))]
                + [slab(s) for s in (0, 1, 2, 3, 4, 5, 6, 7, 8)]
                + hist_specs
                + [pl.BlockSpec((tm, d), lambda i, *_: (i, 0)), gate_spec,
                   const(conv_w.shape), const(w_pa.shape), const(w_pb.shape), const(w_o.shape),
                   const(final_g.shape)])
    operands = [attn, *([rest] * 9), h1, h2, x, gate, conv_w, w_pa, w_pb, w_o, final_g]
    assert len(operands) == N_OUT_INPUTS
    out_specs = [pl.BlockSpec((tm, d), lambda i, *_: (i, 0)), cu_spec]
    out_shape = [jax.ShapeDtypeStruct((m, d), F32), cu_shape]
    scratch, prefetch, static = [], [], None
    if gather is not None:
        picks, pt, pc, pown, l, vnew, cache_v, gather_q, n_pages = gather
        db, rows, _ = pc.shape
        page_size = cache_v.shape[1]
        n_slots = gather_q * MOBA_TOPK * (MOBA_BLOCK // page_size)
        assert tiles == db, "one sample batch rides on each row tile"
        per_b = lambda a: pl.BlockSpec((None,) + a.shape[1:], lambda i, *_: (i, 0, 0))
        in_specs += [per_b(pc), per_b(pown), per_b(l), per_b(vnew), pl.BlockSpec(memory_space=pl.ANY)]
        operands += [pc, pown, l, vnew, cache_v]
        out_specs.append(pl.BlockSpec((None, gather_q, dc), lambda i, *_: (i, 0, 0)))
        out_shape.append(jax.ShapeDtypeStruct((db, gather_q, dc), F32))
        scratch = [pltpu.VMEM((N_HEADS, n_slots, page_size, HEAD_DIM), F32),
                   pltpu.SemaphoreType.DMA((N_HEADS,))]
        prefetch = [picks, pt]
        static = dict(n_slots=n_slots, n_pages=n_pages, page_size=page_size, n_q=gather_q)
    grid_spec = pltpu.PrefetchScalarGridSpec(
        num_scalar_prefetch=len(prefetch), grid=(tiles,), in_specs=in_specs, out_specs=out_specs,
        scratch_shapes=scratch)
    return pl.pallas_call(
        functools.partial(_out_kernel, sample=sample, tiles_per_seq=tiles_per_seq, n_q=n_q,
                          gather=static),
        grid_spec=grid_spec,
        out_shape=out_shape,
        compiler_params=_params("arbitrary"),
        name="out",
    )(*prefetch, *operands)


def kernel(x_prompt, x_sample, cache_k, cache_v, state_conv, page_table, c_prompt, c_sample,
           norm_g, w_ada, b_ada, w_in, conv_w, w_pa, w_pb, w_o, final_g):
    depth = norm_g.shape[0]
    assert depth == 1, "single-layer step only"
    batch, seq, d = x_prompt.shape
    db, n_q, _ = x_sample.shape
    n_pool, page_size, n_heads, head_dim = cache_k.shape[1:]
    assert (n_heads, head_dim) == (N_HEADS, HEAD_DIM)
    n_pages = page_table.shape[1]
    past_len = n_pages * page_size
    d_attn = n_heads * head_dim
    d_conv = d - d_attn
    assert d_attn == d_conv == 1024 and w_in.shape[2] == 12 * 1024
    assert seq % MOBA_BLOCK == 0 and past_len % MOBA_BLOCK == 0 and n_q <= MOBA_BLOCK

    slopes = jnp.exp2(-8.0 * jnp.arange(1, N_HEADS + 1, dtype=F32) / N_HEADS)

    n_c = batch + db
    c_all = jnp.pad(jnp.concatenate([c_prompt, c_sample], axis=0), ((0, -n_c % 8), (0, 0)))
    mod = _ada(c_all, w_ada[0], b_ada[0][None, :])
    mod4 = mod.reshape(mod.shape[0], 3, 1, d)
    mod_tok = jnp.repeat(mod[batch:n_c].reshape(db, 3, d), n_q, axis=0)

    xp = x_prompt.reshape(batch * seq, d)
    xs = x_sample.reshape(db * n_q, d)
    g = norm_g[0][None, :]
    tm_prep = 512
    per_seq = seq // tm_prep
    mod_spec = lambda s: pl.BlockSpec((None, None, 1, d), lambda i: (i // per_seq, s, 0, 0))
    h_p = _prep(xp, g, mod4, mod4, mod_spec(1), mod_spec(0), tm_prep)
    tok_spec = pl.BlockSpec((db * n_q, d), lambda i: (0, 0))
    h_s = _prep(xs, g, mod_tok[:, 1], mod_tok[:, 0], tok_spec, tok_spec, db * n_q)

    w = w_in[0]
    qkv_p, (q_s, k_s, v_s) = _proj(h_p, h_s, w, 0, 3, F32)
    k_p, v_p = qkv_p[1], qkv_p[2]
    rest_p, rest_s = _proj(h_p, h_s, w, 3, 9, BF16)

    ck = cache_k[0].reshape(n_pool, page_size * N_HEADS, HEAD_DIM)
    pt = page_table.reshape(-1).astype(jnp.int32)
    qrows, qpair = _sample_query_rows(q_s, db, n_q)
    attn_p, st, ksum = _attn_p(qkv_p, slopes, batch, seq, pt, qpair, ck, n_pages)
    picks, pc, pown, l, vnew = _sample_probs(qrows, st, ksum, k_s, v_s, slopes, db, n_q)

    cw = conv_w[0]
    wpa, wpb, wo = w_pa[0].astype(BF16), w_pb[0].astype(BF16), w_o[0].astype(BF16)
    fg = final_g[None, :]
    tm_out = batch * seq // db
    assert seq % tm_out == 0 and tm_out % HALO_ROWS == 0
    tiles_per_seq = seq // tm_out
    gate_p_spec = pl.BlockSpec((None, None, 1, d), lambda i, *_: (i // tiles_per_seq, 2, 0, 0))
    y_p, cu_tail, attn_s = _out(
        attn_p, rest_p, rest_p, rest_p, xp, mod4, gate_p_spec, cw, wpa, wpb, wo, fg,
        sample=False, tm=tm_out, tiles_per_seq=tiles_per_seq,
        gather=(picks, pt, pc, pown, l, vnew, cache_v[0], n_q, n_pages))
    attn_s = attn_s.reshape(db * n_q, d_attn)

    state = state_conv[0]
    zeros = jnp.zeros((db, n_q - 1, d_conv), F32)
    hist1 = jnp.concatenate([state[:, 1:2], zeros], axis=1).reshape(db * n_q, d_conv)
    hist2 = jnp.concatenate([state, zeros[:, 1:]], axis=1).reshape(db * n_q, d_conv)
    gate_s_spec = pl.BlockSpec((db * n_q, d), lambda i, *_: (0, 0))
    y_s, cu_s = _out(attn_s, rest_s, hist1, hist2, xs, mod_tok[:, 2], gate_s_spec, cw, wpa, wpb, wo, fg,
                     sample=True, tm=db * n_q, n_q=n_q)

    conv_p = cu_tail.reshape(batch, tiles_per_seq, CONV_WIDTH - 1, d_conv)[:, -1]
    conv_s = cu_s.reshape(db, n_q, d_conv)[:, n_q - (CONV_WIDTH - 1):]
    kv_p = (1, batch, seq, N_HEADS, HEAD_DIM)
    kv_s = (1, db, n_q, N_HEADS, HEAD_DIM)
    return (y_p.reshape(batch, seq, d), y_s.reshape(db, n_q, d),
            k_p.reshape(kv_p), v_p.reshape(kv_p), conv_p[None],
            k_s.reshape(kv_s), v_s.reshape(kv_s), conv_s[None])
```

```python
import functools

import jax
import jax.numpy as jnp
from jax import lax
from jax.experimental import pallas as pl
from jax.experimental.pallas import tpu as pltpu

F32 = jnp.float32
BF16 = jnp.bfloat16

N_HEADS = 8
HEAD_DIM = 128
MOBA_BLOCK = 256
MOBA_TOPK = 3
CONV_WIDTH = 3
NORM_EPS = 1e-6
NEG = float(jnp.finfo(jnp.float32).min)
MASKED = -1e30
ATTN_SCALE = HEAD_DIM ** -0.5
LOG2_E = 1.4426950408889634
SCALE2 = ATTN_SCALE * LOG2_E
Q_TILES = 2

VMEM_LIMIT_BYTES = 56 * 1024 * 1024

_NT = (((1,), (1,)), ((), ()))


def _params(*semantics):
    return pltpu.CompilerParams(dimension_semantics=semantics, vmem_limit_bytes=VMEM_LIMIT_BYTES)


def _ada_kernel(c_ref, w_ref, b_ref, o_ref):
    o_ref[...] = jnp.dot(c_ref[...].astype(BF16), w_ref[...].astype(BF16),
                         preferred_element_type=F32) + b_ref[...]


def _ada(c_all, w_ada, b_ada):
    rows, d = c_all.shape
    n = w_ada.shape[1]
    tn = 1024
    return pl.pallas_call(
        _ada_kernel,
        grid=(n // tn,),
        in_specs=[pl.BlockSpec((rows, d), lambda j: (0, 0)),
                  pl.BlockSpec((d, tn), lambda j: (0, j)),
                  pl.BlockSpec((1, tn), lambda j: (0, j))],
        out_specs=pl.BlockSpec((rows, tn), lambda j: (0, j)),
        out_shape=jax.ShapeDtypeStruct((rows, n), F32),
        compiler_params=_params("arbitrary"),
        name="ada",
    )(c_all, w_ada, b_ada)


def _prep_kernel(x_ref, g_ref, scale_ref, shift_ref, h_ref):
    x = x_ref[...]
    y = x * lax.rsqrt(jnp.mean(x * x, axis=-1, keepdims=True) + NORM_EPS)
    h = (y * g_ref[...]) * (1.0 + scale_ref[...]) + shift_ref[...]
    h_ref[...] = h.astype(BF16)


def _prep(x, g, scale, shift, scale_spec, shift_spec, tm):
    m, d = x.shape
    return pl.pallas_call(
        _prep_kernel,
        grid=(m // tm,),
        in_specs=[pl.BlockSpec((tm, d), lambda i: (i, 0)),
                  pl.BlockSpec((1, d), lambda i: (0, 0)),
                  scale_spec, shift_spec],
        out_specs=pl.BlockSpec((tm, d), lambda i: (i, 0)),
        out_shape=jax.ShapeDtypeStruct((m, d), BF16),
        compiler_params=_params("arbitrary"),
        name="prep",
    )(x, g, scale, shift)


def _proj_kernel(*refs, carry_cast):
    if carry_cast:
        hp_ref, hs_ref, w_ref, wc_ref, op_ref, os_ref, wc_out_ref, wb_ref = refs
        wc_out_ref[...] = wc_ref[...].astype(BF16)
    else:
        hp_ref, hs_ref, w_ref, op_ref, os_ref, wb_ref = refs

    @pl.when(pl.program_id(1) == 0)
    def _():
        wb_ref[...] = w_ref[...].astype(BF16)
        os_ref[...] = jnp.dot(hs_ref[...], wb_ref[...], preferred_element_type=F32).astype(os_ref.dtype)

    op_ref[...] = jnp.dot(hp_ref[...], wb_ref[...], preferred_element_type=F32).astype(op_ref.dtype)


def _proj(h_p, h_s, w, col0, ncol, dtype, tm=1024, tn=1024, cast=None):
    mp, d = h_p.shape
    ms = h_s.shape[0]
    steps = mp // tm
    in_specs = [pl.BlockSpec((tm, d), lambda j, i: (i, 0)),
                pl.BlockSpec((ms, d), lambda j, i: (0, 0)),
                pl.BlockSpec((d, tn), lambda j, i: (0, col0 + j))]
    out_specs = [pl.BlockSpec((None, tm, tn), lambda j, i: (j, i, 0)),
                 pl.BlockSpec((None, ms, tn), lambda j, i: (j, 0, 0))]
    out_shape = [jax.ShapeDtypeStruct((ncol, mp, tn), dtype),
                 jax.ShapeDtypeStruct((ncol, ms, tn), dtype)]
    operands = [h_p, h_s, w]
    if cast is not None:
        rows_c, cols_c = cast.shape
        assert rows_c % (ncol * steps) == 0
        slice_spec = pl.BlockSpec((rows_c // (ncol * steps), cols_c), lambda j, i: (j * steps + i, 0))
        in_specs.append(slice_spec)
        out_specs.append(slice_spec)
        out_shape.append(jax.ShapeDtypeStruct(cast.shape, BF16))
        operands.append(cast)
    return pl.pallas_call(
        functools.partial(_proj_kernel, carry_cast=cast is not None),
        grid=(ncol, steps),
        in_specs=in_specs,
        out_specs=out_specs,
        out_shape=out_shape,
        scratch_shapes=[pltpu.VMEM((d, tn), BF16)],
        compiler_params=_params("arbitrary", "arbitrary"),
        name="proj",
    )(*operands)


def _topk_mask(gate, valid, idx_f, n, axis):
    g = jnp.where(valid, gate, -jnp.inf)
    sel = jnp.zeros(gate.shape, F32)
    for _ in range(MOBA_TOPK):
        m = jnp.max(g, axis=axis, keepdims=True)
        first = jnp.min(jnp.where(g == m, idx_f, float(n)), axis=axis, keepdims=True)
        hit = idx_f == first
        sel = jnp.where(hit & valid, 1.0, sel)
        g = jnp.where(hit, -jnp.inf, g)
    return sel


SUBLANES = 8
ROWS_PER_HEAD = 4
HEAD_PAIR = N_HEADS // 2
PAIRED = N_HEADS // HEAD_PAIR


def _kstream_repack(k_refs, ksum_ref, kb_ref, page_size):
    pages_per_block = MOBA_BLOCK // page_size
    pair_rows = PAIRED * page_size
    row_hh = lax.broadcasted_iota(jnp.int32, (SUBLANES, HEAD_DIM), 0) % PAIRED
    for blk in range(len(k_refs) // pages_per_block):
        for g in range(HEAD_PAIR):
            tot = None
            for j in range(blk * pages_per_block, (blk + 1) * pages_per_block):
                x = k_refs[j][pl.ds(g, pair_rows, stride=HEAD_PAIR), :]
                kb_ref[j * pair_rows:(j + 1) * pair_rows, g * HEAD_DIM:(g + 1) * HEAD_DIM] = (
                    x.astype(BF16))
                part = jnp.sum(x.reshape(pair_rows // SUBLANES, SUBLANES, HEAD_DIM), axis=0)
                tot = part if tot is None else tot + part
            for hh in range(PAIRED):
                head = g + HEAD_PAIR * hh
                ksum_ref[blk:blk + 1, head * HEAD_DIM:(head + 1) * HEAD_DIM] = jnp.sum(
                    jnp.where(row_hh == hh, tot, 0.0), axis=0, keepdims=True)


def _kstream_scores(qpair_ref, kb_ref, st_ref):
    st_ref[...] = lax.dot_general(qpair_ref[...].astype(BF16), kb_ref[...], _NT,
                                  preferred_element_type=F32)


def _attn_p_kernel(pt_ref, slopes_ref, q_ref, k_ref, v_ref, qrows_ref, *refs,
                   n_blocks, n_stream, page_size):
    kp_refs = refs[:n_stream]
    o_ref, st_ref, ksum_ref, ka_ref, vb_ref, kmean_ref, eye_ref, qa_ref, s_ref, kb_ref = refs[n_stream:]
    blk = MOBA_BLOCK
    half = blk // 2
    h = pl.program_id(1)
    group = pl.program_id(2)

    @pl.when(group == 0)
    def _():
        lane = lax.broadcasted_iota(jnp.int32, (blk, HEAD_DIM), 1)
        for n in range(n_blocks):
            rows = slice(n * blk, (n + 1) * blk)
            kf = k_ref[rows, :]
            ka_ref[rows, :HEAD_DIM] = kf.astype(BF16)
            ka_ref[rows, HEAD_DIM:] = jnp.where(lane == n, 1.0, 0.0).astype(BF16)
            vb_ref[rows, :] = v_ref[rows, :].astype(BF16)
            kmean_ref[n:n + 1, :] = jnp.sum(kf, axis=0, keepdims=True) * (1.0 / blk)
        eye_ref[...] = jnp.where(lax.broadcasted_iota(jnp.int32, (blk, blk), 0)
                                 == lax.broadcasted_iota(jnp.int32, (blk, blk), 1),
                                 1.0, 0.0).astype(BF16)

    _kstream_repack(kp_refs, ksum_ref, kb_ref, page_size)

    slope2 = slopes_ref[h] * LOG2_E
    for t in range(Q_TILES):
        qi = group * Q_TILES + t
        qf = q_ref[t * blk:(t + 1) * blk, :]
        gate_t = lax.dot_general(kmean_ref[...], qf, _NT, precision=lax.Precision.HIGHEST,
                                 preferred_element_type=F32)
        row = lax.broadcasted_iota(jnp.int32, gate_t.shape, 0)
        sel_t = _topk_mask(gate_t, row < qi, row.astype(F32), n_blocks, axis=0)
        unpicked_t = jnp.concatenate([jnp.where(sel_t > 0.5, 0.0, MASKED),
                                      jnp.zeros((HEAD_DIM - n_blocks, blk), F32)], axis=0)
        unpicked = lax.dot_general(eye_ref[...], unpicked_t.astype(BF16), _NT,
                                   preferred_element_type=F32)
        qa_ref[t, :, :HEAD_DIM] = qf.astype(BF16)
        qa_ref[t, :, HEAD_DIM:] = unpicked.astype(BF16)
    _kstream_scores(qrows_ref, kb_ref, st_ref)

    def halves(x):
        return x[:, :half], x[:, half:]

    def attend(first_qi):
        key_col = lax.broadcasted_iota(jnp.int32, (1, blk), 1).astype(F32)
        col_bias = slope2 * key_col
        rel = (lax.broadcasted_iota(jnp.int32, (blk, blk), 0)
               - lax.broadcasted_iota(jnp.int32, (blk, blk), 1))
        tiles = [(t, first_qi + t) for t in range(Q_TILES)]

        own, m = {}, {}
        for t, qi in tiles:
            s_own = lax.dot_general(qa_ref[t, :, :HEAD_DIM], ka_ref[qi * blk:(qi + 1) * blk, :HEAD_DIM],
                                    _NT, preferred_element_type=F32) * SCALE2 + col_bias
            own[t] = jnp.where(rel >= 0, s_own, NEG)
            lo, hi = halves(own[t])
            mx = jnp.maximum(lo, hi)
            for n in range(qi):
                bias_n = col_bias + slope2 * float((n - qi) * blk)
                s = lax.dot_general(qa_ref[t], ka_ref[n * blk:(n + 1) * blk, :], _NT,
                                    preferred_element_type=F32) * SCALE2 + bias_n
                s_ref[t, n] = s
                lo, hi = halves(s)
                mx = jnp.maximum(mx, jnp.maximum(lo, hi))
            m[t] = jnp.broadcast_to(jnp.max(mx, axis=1, keepdims=True), (blk, half))

        def probs(s, m_t):
            lo, hi = halves(s)
            lo, hi = jnp.exp2(lo - m_t), jnp.exp2(hi - m_t)
            return lo + hi, jnp.concatenate([lo, hi], axis=1).astype(BF16)

        for t, qi in tiles:
            lsum, p = probs(own[t], m[t])
            acc = jnp.dot(p, vb_ref[qi * blk:(qi + 1) * blk, :], preferred_element_type=F32)
            for n in range(qi):
                part, p = probs(s_ref[t, n], m[t])
                lsum = lsum + part
                acc = acc + jnp.dot(p, vb_ref[n * blk:(n + 1) * blk, :], preferred_element_type=F32)
            l = jnp.sum(lsum, axis=1, keepdims=True)
            o_ref[t * blk:(t + 1) * blk, :] = (acc / l).astype(o_ref.dtype)

    for g in range(n_blocks // Q_TILES):
        pl.when(group == g)(functools.partial(attend, g * Q_TILES))


def _attn_p(q, k, v, slopes, batch, seq, pt, qpair, cache_k, n_pages):
    n_blocks = seq // MOBA_BLOCK
    blk = MOBA_BLOCK
    assert n_blocks % Q_TILES == 0 and n_blocks <= HEAD_DIM
    n_groups = n_blocks // Q_TILES
    db, rows, pair_width = qpair.shape
    width = N_HEADS * HEAD_DIM
    page_size = cache_k.shape[1] // N_HEADS
    n_steps = batch * N_HEADS * n_groups
    n_stream = db * n_pages // n_steps
    assert n_stream * n_steps == db * n_pages and n_pages % n_stream == 0
    assert (n_stream * page_size) % MOBA_BLOCK == 0
    steps_per_sb = n_pages // n_stream
    keys_per_step = n_stream * page_size
    blocks_per_step = keys_per_step // MOBA_BLOCK

    def stream_pos(b, h, qi):
        lin = (b * N_HEADS + h) * n_groups + qi
        return lin // steps_per_sb, lin % steps_per_sb

    def page_spec(j):
        def index(b, h, qi, pt):
            sb, c = stream_pos(b, h, qi)
            return pt[sb * n_pages + c * n_stream + j], 0, 0
        return pl.BlockSpec((None, page_size * N_HEADS, HEAD_DIM), index)

    kv_spec = pl.BlockSpec((seq, HEAD_DIM), lambda b, h, qi, pt: (b, h))
    q_spec = pl.BlockSpec((Q_TILES * blk, HEAD_DIM), lambda b, h, qi, pt: (b * n_groups + qi, h))
    grid_spec = pltpu.PrefetchScalarGridSpec(
        num_scalar_prefetch=1,
        grid=(batch, N_HEADS, n_groups),
        in_specs=[pl.BlockSpec(memory_space=pltpu.SMEM), q_spec, kv_spec, kv_spec,
                  pl.BlockSpec((None, rows, pair_width),
                               lambda b, h, qi, pt: (stream_pos(b, h, qi)[0], 0, 0))]
        + [page_spec(j) for j in range(n_stream)],
        out_specs=[q_spec,
                   pl.BlockSpec((None, rows, PAIRED * keys_per_step),
                                lambda b, h, qi, pt: (stream_pos(b, h, qi)[0], 0, stream_pos(b, h, qi)[1])),
                   pl.BlockSpec((None, None, blocks_per_step, width),
                                lambda b, h, qi, pt: stream_pos(b, h, qi) + (0, 0))],
        scratch_shapes=[pltpu.VMEM((seq, 2 * HEAD_DIM), BF16),
                        pltpu.VMEM((seq, HEAD_DIM), BF16),
                        pltpu.VMEM((n_blocks, HEAD_DIM), F32),
                        pltpu.VMEM((blk, blk), BF16),
                        pltpu.VMEM((Q_TILES, blk, 2 * HEAD_DIM), BF16),
                        pltpu.VMEM((Q_TILES, n_blocks, blk, blk), F32),
                        pltpu.VMEM((PAIRED * keys_per_step, pair_width), BF16)],
    )
    attn, st, ksum = pl.pallas_call(
        functools.partial(_attn_p_kernel, n_blocks=n_blocks, n_stream=n_stream, page_size=page_size),
        grid_spec=grid_spec,
        out_shape=[jax.ShapeDtypeStruct(q.shape, BF16),
                   jax.ShapeDtypeStruct((db, rows, PAIRED * n_pages * page_size), F32),
                   jax.ShapeDtypeStruct((db, steps_per_sb, blocks_per_step, width), F32)],
        compiler_params=_params("arbitrary", "arbitrary", "arbitrary"),
        name="attn_p",
    )(pt, slopes, q, k, v, qpair, *([cache_k] * n_stream))
    return attn, st, ksum.reshape(db, steps_per_sb * blocks_per_step, width)


def _dot_onehot_exact(x, onehot):
    hi = x.astype(BF16)
    rest = x - hi.astype(F32)
    mid = rest.astype(BF16)
    lo = (rest - mid.astype(F32)).astype(BF16)
    return ((jnp.dot(hi, onehot, preferred_element_type=F32)
             + jnp.dot(mid, onehot, preferred_element_type=F32))
            + jnp.dot(lo, onehot, preferred_element_type=F32))


def _select_kernel(qrows_ref, ksum_ref, st_ref, knew_ref, srow_ref,
                   pc_ref, pown_ref, l_ref, ids_ref, *, n_new):
    rows = st_ref.shape[0]
    n_blocks = ksum_ref.shape[0]
    blk = MOBA_BLOCK
    past_len = n_blocks * blk
    pblk = PAIRED * blk
    qr = qrows_ref[...]
    kmean = ksum_ref[...] * (1.0 / blk)
    gate = lax.dot_general(qr, kmean, _NT, precision=lax.Precision.HIGHEST,
                           preferred_element_type=F32)
    lane_f = lax.broadcasted_iota(jnp.int32, gate.shape, 1).astype(F32)
    expand = (lax.broadcasted_iota(jnp.int32, (n_blocks, n_blocks * HEAD_DIM), 1) // HEAD_DIM
              == lax.broadcasted_iota(jnp.int32, (n_blocks, n_blocks * HEAD_DIM), 0)).astype(BF16)
    own_col = (lax.broadcasted_iota(jnp.int32, (rows, pblk), 1) % PAIRED
               == lax.broadcasted_iota(jnp.int32, (rows, pblk), 0) // (ROWS_PER_HEAD * HEAD_PAIR))
    compact = (lax.broadcasted_iota(jnp.int32, (pblk, blk), 0) // PAIRED
               == lax.broadcasted_iota(jnp.int32, (pblk, blk), 1)).astype(BF16)

    slope = srow_ref[:, 0:1]
    qq = lax.broadcasted_iota(jnp.int32, (rows, 1), 0) % ROWS_PER_HEAD
    q_pos = (past_len + qq).astype(F32)
    col = lax.broadcasted_iota(jnp.int32, (1, blk), 1).astype(F32)

    g = gate
    firsts, keeps = [], []
    for _ in range(MOBA_TOPK):
        m = jnp.max(g, axis=1, keepdims=True)
        first = jnp.min(jnp.where(g == m, lane_f, float(n_blocks)), axis=1, keepdims=True)
        hit = lane_f == first
        g = jnp.where(hit, -jnp.inf, g)
        firsts.append(first)
        keeps.append(jnp.dot(jnp.where(hit, 1.0, 0.0).astype(BF16), expand,
                             preferred_element_type=F32) > 0.5)

    lane_tiles = pblk // HEAD_DIM
    picked = [[jnp.zeros((rows, HEAD_DIM), F32)] * lane_tiles for _ in range(MOBA_TOPK)]
    for n in range(n_blocks):
        for v in range(lane_tiles):
            st_nv = st_ref[:, n * pblk + v * HEAD_DIM:n * pblk + (v + 1) * HEAD_DIM]
            for t in range(MOBA_TOPK):
                flag = keeps[t][:, n * HEAD_DIM:(n + 1) * HEAD_DIM]
                picked[t][v] = jnp.where(flag, st_nv, picked[t][v])

    scores = []
    for t in range(MOBA_TOPK):
        sc = jnp.concatenate(picked[t], axis=1)
        sc = _dot_onehot_exact(jnp.where(own_col, sc, 0.0), compact)
        dist = (q_pos - firsts[t] * float(blk)) - col
        scores.append(sc * ATTN_SCALE - slope * dist)

    j = lax.broadcasted_iota(jnp.int32, (rows, knew_ref.shape[0]), 1)
    so = lax.dot_general(qr.astype(BF16), knew_ref[...].astype(BF16), _NT,
                         preferred_element_type=F32) * ATTN_SCALE
    so = so - slope * (qq - j).astype(F32)
    so = jnp.where((j <= qq) & (j < n_new), so, NEG)

    m = jnp.max(so, axis=1, keepdims=True)
    for s in scores:
        m = jnp.maximum(m, jnp.max(s, axis=1, keepdims=True))
    po = jnp.exp(so - m)
    l = jnp.sum(po, axis=1, keepdims=True)
    for t, s in enumerate(scores):
        p = jnp.exp(s - m)
        l = l + jnp.sum(p, axis=1, keepdims=True)
        pc_ref[:, t * blk:(t + 1) * blk] = p
    pown_ref[...] = po
    l_ref[...] = jnp.broadcast_to(l, l_ref.shape)
    out_lane = lax.broadcasted_iota(jnp.int32, ids_ref.shape, 1)
    ids = jnp.zeros(ids_ref.shape, F32)
    for t, first in enumerate(firsts):
        ids = jnp.where(out_lane == t, first, ids)
    ids_ref[...] = ids.astype(jnp.int32)


def _select(qrows, ksum, st, knew, srow, n_new):
    db, rows, width = qrows.shape
    n_blocks = ksum.shape[1]
    past_len = st.shape[2]
    pad = knew.shape[1]
    assert n_blocks >= MOBA_TOPK and past_len == PAIRED * n_blocks * MOBA_BLOCK
    per_b = lambda *shape: pl.BlockSpec((None,) + shape, lambda b: (b,) + (0,) * len(shape))
    return pl.pallas_call(
        functools.partial(_select_kernel, n_new=n_new),
        grid=(db,),
        in_specs=[per_b(rows, width), per_b(n_blocks, width), per_b(rows, past_len),
                  per_b(pad, width), pl.BlockSpec((rows, HEAD_DIM), lambda b: (0, 0))],
        out_specs=[per_b(rows, MOBA_TOPK * MOBA_BLOCK), per_b(rows, pad), per_b(rows, HEAD_DIM),
                   per_b(rows, HEAD_DIM)],
        out_shape=[jax.ShapeDtypeStruct((db, rows, MOBA_TOPK * MOBA_BLOCK), F32),
                   jax.ShapeDtypeStruct((db, rows, pad), F32),
                   jax.ShapeDtypeStruct((db, rows, HEAD_DIM), F32),
                   jax.ShapeDtypeStruct((db, rows, HEAD_DIM), jnp.int32)],
        compiler_params=_params("arbitrary"),
        name="select",
    )(qrows, ksum, st, knew, srow)


def _gather_copy(picks_ref, pt_ref, cache_ref, v_buf, sem, sb, h, s, *, n_slots, n_pages, page_size):
    pages_per_block = MOBA_BLOCK // page_size
    block = picks_ref[(sb * N_HEADS + h) * (n_slots // pages_per_block) + s // pages_per_block]
    page = pt_ref[sb * n_pages + block * pages_per_block + s % pages_per_block]
    return pltpu.make_async_copy(cache_ref.at[page, :, h, :], v_buf.at[h, s], sem.at[h])


def _gather_attend(h, pc_ref, pown_ref, l_ref, vnew_ref, v_buf, o_ref, *, n_q, n_slots):
    page_size = v_buf.shape[2]
    heads_per_tile = SUBLANES // ROWS_PER_HEAD
    tile_rows = slice((h // heads_per_tile) * SUBLANES, (h // heads_per_tile + 1) * SUBLANES)
    base = (h % heads_per_tile) * ROWS_PER_HEAD
    cols = slice(h * HEAD_DIM, (h + 1) * HEAD_DIM)
    pc = pc_ref[tile_rows, :]
    row = lax.broadcasted_iota(jnp.int32, pc.shape, 0)
    p_all = jnp.concatenate([jnp.where(row == base + q, pc, 0.0) for q in range(n_q)], axis=1)
    v_all = v_buf[h].reshape(n_slots * page_size, HEAD_DIM)
    acc = (jnp.dot(pown_ref[tile_rows, :].astype(BF16), vnew_ref[:, cols].astype(BF16),
                   preferred_element_type=F32)
           + jnp.dot(p_all.astype(BF16), v_all.astype(BF16), preferred_element_type=F32))
    out = acc / l_ref[tile_rows, 0:1]
    o_ref[:, cols] = out[base:base + n_q, :]


def _sample_query_rows(q_s, db, n_q):
    assert n_q <= ROWS_PER_HEAD
    rows = ROWS_PER_HEAD * N_HEADS
    q4 = jnp.pad(q_s.reshape(db, n_q, N_HEADS, HEAD_DIM).transpose(0, 2, 1, 3),
                 ((0, 0), (0, 0), (0, ROWS_PER_HEAD - n_q), (0, 0)))
    place = lambda onehot: (q4[:, :, :, None, :] * onehot[None, :, None, :, None]
                            ).reshape(db, rows, onehot.shape[1] * HEAD_DIM)
    heads = jnp.arange(N_HEADS)
    own = (heads[:, None] == heads[None, :]).astype(F32)
    pair = (heads[:, None] % HEAD_PAIR == jnp.arange(HEAD_PAIR)[None, :]).astype(F32)
    return place(own), place(pair)


def _sample_probs(qrows, st, ksum, k_s, v_s, slopes, db, n_q):
    d_attn = N_HEADS * HEAD_DIM
    rows = ROWS_PER_HEAD * N_HEADS
    pad_new = ((0, 0), (0, -n_q % SUBLANES), (0, 0))
    knew = jnp.pad(k_s.reshape(db, n_q, d_attn), pad_new)
    vnew = jnp.pad(v_s.reshape(db, n_q, d_attn), pad_new)
    srow = jnp.broadcast_to(jnp.repeat(slopes, ROWS_PER_HEAD)[:, None], (rows, HEAD_DIM))
    pc, pown, l, ids = _select(qrows, ksum, st, knew, srow, n_q)
    picks = ids[:, :, :MOBA_TOPK].reshape(db, N_HEADS, ROWS_PER_HEAD, MOBA_TOPK)[:, :, :n_q]
    return picks.reshape(-1), pc, pown, l, vnew


HALO_ROWS = 16


N_OUT_INPUTS = 19


def _out_kernel(*refs, sample, tiles_per_seq, n_q, gather):
    if gather is not None:
        picks_ref, pt_ref = refs[:2]
        refs = refs[2:]
        pc_ref, pown_ref, l_ref, vnew_ref, cache_ref = refs[N_OUT_INPUTS:N_OUT_INPUTS + 5]
        y_ref, cu_ref, attn_s_ref, v_buf, sem = refs[N_OUT_INPUTS + 5:]
        sb = pl.program_id(0)
        copy = functools.partial(_gather_copy, picks_ref, pt_ref, cache_ref, v_buf, sem,
                                 n_slots=gather["n_slots"], n_pages=gather["n_pages"],
                                 page_size=gather["page_size"])

        @pl.when(sb == 0)
        def _():
            for h in range(N_HEADS):
                for s in range(gather["n_slots"]):
                    copy(sb, h, s).start()
    else:
        y_ref, cu_ref = refs[N_OUT_INPUTS:]
    (attn_ref, za_ref, bg_ref, cg_ref, u_ref, zb_ref, ga0_ref, ga1_ref, gb0_ref, gb1_ref,
     h1_ref, h2_ref, x_ref, gate_ref, cw_ref, wpa_ref, wpb_ref, wo_ref, fg_ref) = refs[:N_OUT_INPUTS]
    tm = x_ref.shape[0]
    cu = cg_ref[...].astype(F32) * u_ref[...].astype(F32)
    row = lax.broadcasted_iota(jnp.int32, cu.shape, 0)
    prev1 = pltpu.roll(cu, 1, 0)
    prev2 = pltpu.roll(cu, 2, 0)
    if sample:
        t = row % n_q
        prev1 = jnp.where(t >= 1, prev1, h1_ref[...])
        prev2 = jnp.where(t >= 2, prev2, h2_ref[...])
        cu_ref[...] = cu
    else:
        first = (pl.program_id(0) % tiles_per_seq) == 0
        halo = h1_ref[...].astype(F32) * h2_ref[...].astype(F32)
        halo = jnp.where(first, 0.0, halo)
        before1 = halo[HALO_ROWS - 1:HALO_ROWS, :]
        before2 = halo[HALO_ROWS - 2:HALO_ROWS - 1, :]
        prev1 = jnp.where(row >= 1, prev1, before1)
        prev2 = jnp.where(row >= 2, prev2, jnp.where(row == 0, before2, before1))
        cu_ref[...] = cu[tm - (CONV_WIDTH - 1):, :]

    conv = bg_ref[...].astype(F32) * (cw_ref[0:1, :] * prev2 + cw_ref[1:2, :] * prev1
                                      + cw_ref[2:3, :] * cu)
    a_act = (attn_ref[...].astype(F32) * jax.nn.silu(za_ref[...].astype(F32))).astype(BF16)
    b_act = (conv * jax.nn.silu(zb_ref[...].astype(F32))).astype(BF16)
    ya = jnp.dot(a_act, wpa_ref[...], preferred_element_type=F32)
    yb = jnp.dot(b_act, wpb_ref[...], preferred_element_type=F32)
    ga = jnp.concatenate([ga0_ref[...], ga1_ref[...]], axis=1).astype(F32)
    gb = jnp.concatenate([gb0_ref[...], gb1_ref[...]], axis=1).astype(F32)
    merged = jax.nn.sigmoid(ga) * ya + jax.nn.sigmoid(gb) * yb
    out = jnp.dot(merged.astype(BF16), wo_ref[...], preferred_element_type=F32)
    xn = x_ref[...] + gate_ref[...] * out
    y = xn * lax.rsqrt(jnp.mean(xn * xn, axis=-1, keepdims=True) + NORM_EPS)
    y_ref[...] = y * fg_ref[...]

    if gather is not None:
        for h in range(N_HEADS):
            for s in range(gather["n_slots"]):
                copy(sb, h, s).wait()
            _gather_attend(h, pc_ref, pown_ref, l_ref, vnew_ref, v_buf, attn_s_ref,
                           n_q=gather["n_q"], n_slots=gather["n_slots"])

            @pl.when(sb + 1 < pl.num_programs(0))
            def _():
                for s in range(gather["n_slots"]):
                    copy(sb + 1, h, s).start()


def _out(attn, rest, h1, h2, x, gate, gate_spec, conv_w, w_pa, w_pb, w_o, final_g,
         *, sample, tm, tiles_per_seq=1, n_q=1, gather=None):
    m, d = x.shape
    dc = attn.shape[1]
    tiles = m // tm
    slab = lambda s: pl.BlockSpec((None, tm, dc), lambda i, *_: (s, i, 0))
    const = lambda shape: pl.BlockSpec(shape, lambda i, *_: (0,) * len(shape),
                                       pipeline_mode=pl.Buffered(1))
    if sample:
        hist_specs = [pl.BlockSpec((tm, dc), lambda i, *_: (i, 0))] * 2
        cu_spec = pl.BlockSpec((tm, dc), lambda i, *_: (i, 0))
        cu_shape = jax.ShapeDtypeStruct((m, dc), F32)
    else:
        per_tile = tm // HALO_ROWS
        halo = lambda s: pl.BlockSpec((None, HALO_ROWS, dc),
                                      lambda i, *_: (s, jnp.maximum(i * per_tile - 1, 0), 0))
        hist_specs = [halo(2), halo(3)]
        cu_spec = pl.BlockSpec((None, CONV_WIDTH - 1, dc), lambda i, *_: (i, 0, 0))
        cu_shape = jax.ShapeDtypeStruct((tiles, CONV_WIDTH - 1, dc), F32)
    in_specs = ([pl.BlockSpec((tm, dc), lambda i, *_: (i, 0))]
                + [slab(s) for s in (0, 1, 2, 3, 4, 5, 6, 7, 8)]
                + hist_specs
                + [pl.BlockSpec((tm, d), lambda i, *_: (i, 0)), gate_spec,
                   const(conv_w.shape), const(w_pa.shape), const(w_pb.shape), const(w_o.shape),
                   const(final_g.shape)])
    operands = [attn, *([rest] * 9), h1, h2, x, gate, conv_w, w_pa, w_pb, w_o, final_g]
    assert len(operands) == N_OUT_INPUTS
    out_specs = [pl.BlockSpec((tm, d), lambda i, *_: (i, 0)), cu_spec]
    out_shape = [jax.ShapeDtypeStruct((m, d), F32), cu_shape]
    scratch, prefetch, static = [], [], None
    if gather is not None:
        picks, pt, pc, pown, l, vnew, cache_v, gather_q, n_pages = gather
        db, rows, _ = pc.shape
        page_size = cache_v.shape[1]
        n_slots = gather_q * MOBA_TOPK * (MOBA_BLOCK // page_size)
        assert tiles == db, "one sample batch rides on each row tile"
        per_b = lambda a: pl.BlockSpec((None,) + a.shape[1:], lambda i, *_: (i, 0, 0))
        in_specs += [per_b(pc), per_b(pown), per_b(l), per_b(vnew), pl.BlockSpec(memory_space=pl.ANY)]
        operands += [pc, pown, l, vnew, cache_v]
        out_specs.append(pl.BlockSpec((None, gather_q, dc), lambda i, *_: (i, 0, 0)))
        out_shape.append(jax.ShapeDtypeStruct((db, gather_q, dc), F32))
        scratch = [pltpu.VMEM((N_HEADS, n_slots, page_size, HEAD_DIM), F32),
                   pltpu.SemaphoreType.DMA((N_HEADS,))]
        prefetch = [picks, pt]
        static = dict(n_slots=n_slots, n_pages=n_pages, page_size=page_size, n_q=gather_q)
    grid_spec = pltpu.PrefetchScalarGridSpec(
        num_scalar_prefetch=len(prefetch), grid=(tiles,), in_specs=in_specs, out_specs=out_specs,
        scratch_shapes=scratch)
    return pl.pallas_call(
        functools.partial(_out_kernel, sample=sample, tiles_per_seq=tiles_per_seq, n_q=n_q,
                          gather=static),
        grid_spec=grid_spec,
        out_shape=out_shape,
        compiler_params=_params("arbitrary"),
        name="out",
    )(*prefetch, *operands)


def kernel(x_prompt, x_sample, cache_k, cache_v, state_conv, page_table, c_prompt, c_sample,
           norm_g, w_ada, b_ada, w_in, conv_w, w_pa, w_pb, w_o, final_g):
    depth = norm_g.shape[0]
    assert depth == 1, "single-layer step only"
    batch, seq, d = x_prompt.shape
    db, n_q, _ = x_sample.shape
    n_pool, page_size, n_heads, head_dim = cache_k.shape[1:]
    assert (n_heads, head_dim) == (N_HEADS, HEAD_DIM)
    n_pages = page_table.shape[1]
    past_len = n_pages * page_size
    d_attn = n_heads * head_dim
    d_conv = d - d_attn
    assert d_attn == d_conv == 1024 and w_in.shape[2] == 12 * 1024
    assert seq % MOBA_BLOCK == 0 and past_len % MOBA_BLOCK == 0 and n_q <= MOBA_BLOCK

    slopes = jnp.exp2(-8.0 * jnp.arange(1, N_HEADS + 1, dtype=F32) / N_HEADS)

    n_c = batch + db
    c_all = jnp.pad(jnp.concatenate([c_prompt, c_sample], axis=0), ((0, -n_c % 8), (0, 0)))
    mod = _ada(c_all, w_ada[0], b_ada[0][None, :])
    mod4 = mod.reshape(mod.shape[0], 3, 1, d)
    mod_tok = jnp.repeat(mod[batch:n_c].reshape(db, 3, d), n_q, axis=0)

    xp = x_prompt.reshape(batch * seq, d)
    xs = x_sample.reshape(db * n_q, d)
    g = norm_g[0][None, :]
    tm_prep = 512
    per_seq = seq // tm_prep
    mod_spec = lambda s: pl.BlockSpec((None, None, 1, d), lambda i: (i // per_seq, s, 0, 0))
    h_p = _prep(xp, g, mod4, mod4, mod_spec(1), mod_spec(0), tm_prep)
    tok_spec = pl.BlockSpec((db * n_q, d), lambda i: (0, 0))
    h_s = _prep(xs, g, mod_tok[:, 1], mod_tok[:, 0], tok_spec, tok_spec, db * n_q)

    w = w_in[0]
    (q_p,), (q_s,), wpa = _proj(h_p, h_s, w, 0, 1, F32, cast=w_pa[0])
    (k_p,), (k_s,), wpb = _proj(h_p, h_s, w, 1, 1, F32, cast=w_pb[0])
    (v_p,), (v_s,), wo = _proj(h_p, h_s, w, 2, 1, F32, cast=w_o[0])
    rest_p, rest_s = _proj(h_p, h_s, w, 3, 9, BF16)

    ck = cache_k[0].reshape(n_pool, page_size * N_HEADS, HEAD_DIM)
    pt = page_table.reshape(-1).astype(jnp.int32)
    qrows, qpair = _sample_query_rows(q_s, db, n_q)
    attn_p, st, ksum = _attn_p(q_p, k_p, v_p, slopes, batch, seq, pt, qpair, ck, n_pages)
    picks, pc, pown, l, vnew = _sample_probs(qrows, st, ksum, k_s, v_s, slopes, db, n_q)

    cw = conv_w[0]
    fg = final_g[None, :]
    tm_out = batch * seq // db
    assert seq % tm_out == 0 and tm_out % HALO_ROWS == 0
    tiles_per_seq = seq // tm_out
    gate_p_spec = pl.BlockSpec((None, None, 1, d), lambda i, *_: (i // tiles_per_seq, 2, 0, 0))
    y_p, cu_tail, attn_s = _out(
        attn_p, rest_p, rest_p, rest_p, xp, mod4, gate_p_spec, cw, wpa, wpb, wo, fg,
        sample=False, tm=tm_out, tiles_per_seq=tiles_per_seq,
        gather=(picks, pt, pc, pown, l, vnew, cache_v[0], n_q, n_pages))
    attn_s = attn_s.reshape(db * n_q, d_attn)

    state = state_conv[0]
    zeros = jnp.zeros((db, n_q - 1, d_conv), F32)
    hist1 = jnp.concatenate([state[:, 1:2], zeros], axis=1).reshape(db * n_q, d_conv)
    hist2 = jnp.concatenate([state, zeros[:, 1:]], axis=1).reshape(db * n_q, d_conv)
    gate_s_spec = pl.BlockSpec((db * n_q, d), lambda i, *_: (0, 0))
    y_s, cu_s = _out(attn_s, rest_s, hist1, hist2, xs, mod_tok[:, 2], gate_s_spec, cw, wpa, wpb, wo, fg,
                     sample=True, tm=db * n_q, n_q=n_q)

    conv_p = cu_tail.reshape(batch, tiles_per_seq, CONV_WIDTH - 1, d_conv)[:, -1]
    conv_s = cu_s.reshape(db, n_q, d_conv)[:, n_q - (CONV_WIDTH - 1):]
    kv_p = (1, batch, seq, N_HEADS, HEAD_DIM)
    kv_s = (1, db, n_q, N_HEADS, HEAD_DIM)
    return (y_p.reshape(batch, seq, d), y_s.reshape(db, n_q, d),
            k_p.reshape(kv_p), v_p.reshape(kv_p), conv_p[None],
            k_s.reshape(kv_s), v_s.reshape(kv_s), conv_s[None])
```
